```python
import math
import jax, jax.numpy as jnp
from jax import lax
import numpy as np


D_MODEL = 2048
BATCH = 1
SEQ = 16384
DEPTH = 2

CHUNK = 64
MIX_WIDTH = D_MODEL
WIDTH_A = MIX_WIDTH // 2
WIDTH_B = MIX_WIDTH - WIDTH_A
IN_WIDTH = 2 * WIDTH_A + WIDTH_B
GMLP_BLOCK = 128
GMLP_HEAD_DIM = 128
GMLP_HEADS = WIDTH_A // GMLP_HEAD_DIM
SSM_GROUP = 16
SSM_GROUPS = WIDTH_B // SSM_GROUP
SSM_STATE = 64
DT_MIN = 1e-3
DT_MAX = 1e-1
PEER_HEADS = 8
PEER_KEYS = 128
PEER_EXPERTS = PEER_KEYS * PEER_KEYS
PEER_TOPK = 16
PEER_QDIM = 256
PEER_HALF = PEER_QDIM // 2
PEER_TOKEN_BLOCK = 128
N_MOD = 6
EPS = 1e-6

kernel_name = "hymba_gmlp_s5_peer_adaln_block"


def _rmsnorm(x, gain):
    xf = x.astype(jnp.float32)
    y = xf * lax.rsqrt(jnp.mean(xf * xf, axis=-1, keepdims=True) + EPS)
    return (y * gain.astype(jnp.float32)).astype(x.dtype)


def _layernorm(x, gain, bias):
    xf = x.astype(jnp.float32)
    mu = jnp.mean(xf, axis=-1, keepdims=True)
    var = jnp.mean(jnp.square(xf - mu), axis=-1, keepdims=True)
    y = (xf - mu) * lax.rsqrt(var + EPS)
    return (y * gain.astype(jnp.float32) + bias.astype(jnp.float32)).astype(x.dtype)


def _chunk_causal_mask(n):
    blk = jnp.arange(n) // CHUNK
    return blk[:, None] >= blk[None, :]


def _spatial_gating(uv, ln_g, ln_b, w_s, b_s):
    b, l, _ = uv.shape
    z = jax.nn.gelu(uv)
    u, v = z[..., :WIDTH_A], z[..., WIDTH_A:]
    v = _layernorm(v, ln_g, ln_b)
    v = v.reshape(b, l // GMLP_BLOCK, GMLP_BLOCK, GMLP_HEADS, GMLP_HEAD_DIM)
    w = jnp.where(_chunk_causal_mask(GMLP_BLOCK)[None], w_s, 0.0)
    s = jnp.einsum('hij,bnjhc->bnihc', w, v) + b_s.T[None, None, :, :, None]
    return u * s.reshape(b, l, WIDTH_A)


def _ssm_combine(e1, e2):
    a1, s1 = e1
    a2, s2 = e2
    return a1 * a2, a2 * s1 + s2


def _s5(xb, lam_re, lam_im, log_dt, b_re, b_im, c_re, c_im, d_skip, glu_w, glu_b):
    b, l, _ = xb.shape
    out_dtype = xb.dtype
    f32 = jnp.float32
    x = xb.astype(f32).reshape(b, l, SSM_GROUPS, SSM_GROUP)
    lam = lax.complex(lam_re.astype(f32), lam_im.astype(f32))
    dt = jnp.exp(log_dt.astype(f32))[:, None]
    a_bar = jnp.exp(lam * dt)
    bmat = lax.complex(b_re.astype(f32), b_im.astype(f32))
    b_bar = ((a_bar - 1.0) / lam)[..., None] * bmat
    cmat = lax.complex(c_re.astype(f32), c_im.astype(f32))
    steps = jnp.arange(1, CHUNK + 1, dtype=f32)
    a_pow = jnp.exp(lam[None] * (dt[None] * steps[:, None, None]))
    xc = x.reshape(b, l // CHUNK, CHUNK, SSM_GROUPS, SSM_GROUP).transpose(1, 0, 2, 3, 4)

    def chunk_step(h, xk):
        bu = jnp.einsum('bcgh,gph->bcgp', xk.astype(jnp.complex64), b_bar)
        a = jnp.broadcast_to(a_bar, bu.shape)
        _, s = lax.associative_scan(_ssm_combine, (a, bu), axis=1)
        s = s + a_pow[None] * h[:, None]
        y = jnp.einsum('bcgp,ghp->bcgh', s, cmat).real
        return s[:, -1], y

    h0 = jnp.zeros((b, SSM_GROUPS, SSM_STATE), jnp.complex64)
    _, y = lax.scan(chunk_step, h0, xc)
    y = y.transpose(1, 0, 2, 3, 4).reshape(b, l, SSM_GROUPS, SSM_GROUP)
    y = y + d_skip.astype(f32).reshape(SSM_GROUPS, SSM_GROUP) * x
    g = jax.nn.gelu(y.reshape(b, l, WIDTH_B))
    out = g * jax.nn.sigmoid(g @ glu_w.astype(f32) + glu_b.astype(f32))
    return out.astype(out_dtype)


def _peer(h, w_q, sub_keys, u_tab, v_tab):
    b, l, d = h.shape
    tokens = h.reshape(-1, PEER_TOKEN_BLOCK, d)

    def block(hb):
        t = hb.shape[0]
        q = (hb @ w_q).reshape(t, PEER_HEADS, 2, PEER_HALF)
        scores = jnp.einsum('thsc,hsnc->thsn', q, sub_keys).astype(jnp.float32)
        top_s, top_i = lax.top_k(scores, PEER_TOPK)
        cand = (top_s[:, :, 0, :, None] + top_s[:, :, 1, None, :]).reshape(t, PEER_HEADS, PEER_TOPK * PEER_TOPK)
        cand_id = (top_i[:, :, 0, :, None] * PEER_KEYS + top_i[:, :, 1, None, :]).reshape(t, PEER_HEADS, PEER_TOPK * PEER_TOPK)
        best_s, best_j = lax.top_k(cand, PEER_TOPK)
        expert = jnp.take_along_axis(cand_id, best_j, axis=-1)
        gate = jax.nn.softmax(best_s, axis=-1)
        u = jnp.take(u_tab, expert, axis=0)
        v = jnp.take(v_tab, expert, axis=0)
        act = jax.nn.gelu(jnp.einsum('thkd,td->thk', u, hb).astype(jnp.float32))
        coef = (gate * act).astype(hb.dtype)
        return jnp.einsum('thk,thkd->td', coef, v)

    out = lax.map(block, tokens)
    return out.reshape(b, l, d)


def setup_inputs(seed: int = 0) -> dict:
    key = jax.random.key(seed)
    ks = jax.random.split(key, 29)
    f32 = jnp.float32
    nrm = lambda k, shape, std: jax.random.normal(k, shape, f32) * std
    d = D_MODEL
    lam_im_base = jnp.pi * jnp.arange(SSM_STATE, dtype=f32)
    return {
        'x': nrm(ks[0], (BATCH, SEQ, d), 1.0),
        'c': nrm(ks[1], (BATCH, d), 1.0),
        'w_mod': nrm(ks[2], (DEPTH, d, N_MOD * d), 0.5 * d ** -0.5),
        'b_mod': nrm(ks[3], (DEPTH, N_MOD * d), 0.02),
        'norm1_g': 1.0 + nrm(ks[4], (DEPTH, d), 0.02),
        'w_in': nrm(ks[5], (DEPTH, d, IN_WIDTH), d ** -0.5),
        'b_in': nrm(ks[6], (DEPTH, IN_WIDTH), 0.02),
        'gm_ln_g': 1.0 + nrm(ks[7], (DEPTH, WIDTH_A), 0.02),
        'gm_ln_b': nrm(ks[8], (DEPTH, WIDTH_A), 0.02),
        'gm_ws': nrm(ks[9], (DEPTH, GMLP_HEADS, GMLP_BLOCK, GMLP_BLOCK), GMLP_BLOCK ** -0.5),
        'gm_bs': 1.0 + nrm(ks[10], (DEPTH, GMLP_HEADS, GMLP_BLOCK), 0.02),
        'ssm_lam_re': -0.5 + nrm(ks[11], (DEPTH, SSM_GROUPS, SSM_STATE), 0.01),
        'ssm_lam_im': lam_im_base + nrm(ks[12], (DEPTH, SSM_GROUPS, SSM_STATE), 0.01),
        'ssm_log_dt': jax.random.uniform(ks[13], (DEPTH, SSM_GROUPS), f32, math.log(DT_MIN), math.log(DT_MAX)),
        'ssm_b_re': nrm(ks[14], (DEPTH, SSM_GROUPS, SSM_STATE, SSM_GROUP), (2 * SSM_GROUP) ** -0.5),
        'ssm_b_im': nrm(ks[15], (DEPTH, SSM_GROUPS, SSM_STATE, SSM_GROUP), (2 * SSM_GROUP) ** -0.5),
        'ssm_c_re': nrm(ks[16], (DEPTH, SSM_GROUPS, SSM_GROUP, SSM_STATE), (2 * SSM_STATE) ** -0.5),
        'ssm_c_im': nrm(ks[17], (DEPTH, SSM_GROUPS, SSM_GROUP, SSM_STATE), (2 * SSM_STATE) ** -0.5),
        'ssm_d': nrm(ks[18], (DEPTH, WIDTH_B), 1.0),
        'glu_w': nrm(ks[19], (DEPTH, WIDTH_B, WIDTH_B), WIDTH_B ** -0.5),
        'glu_b': nrm(ks[20], (DEPTH, WIDTH_B), 0.02),
        'branch_g': 1.0 + nrm(ks[21], (DEPTH, MIX_WIDTH), 0.02),
        'w_out': nrm(ks[22], (DEPTH, MIX_WIDTH, d), MIX_WIDTH ** -0.5),
        'norm2_g': 1.0 + nrm(ks[23], (DEPTH, d), 0.02),
        'peer_wq': nrm(ks[24], (DEPTH, d, PEER_HEADS * PEER_QDIM), d ** -0.5),
        'peer_keys': nrm(ks[25], (DEPTH, PEER_HEADS, 2, PEER_KEYS, PEER_HALF), PEER_HALF ** -0.5),
        'peer_u': nrm(ks[26], (DEPTH, PEER_EXPERTS, d), d ** -0.5),
        'peer_v': nrm(ks[27], (DEPTH, PEER_EXPERTS, d), 1.0),
        'final_g': 1.0 + nrm(ks[28], (d,), 0.02),
    }


def reference(x, c, w_mod, b_mod, norm1_g, w_in, b_in, gm_ln_g, gm_ln_b, gm_ws, gm_bs,
              ssm_lam_re, ssm_lam_im, ssm_log_dt, ssm_b_re, ssm_b_im, ssm_c_re, ssm_c_im,
              ssm_d, glu_w, glu_b, branch_g, w_out, norm2_g, peer_wq, peer_keys, peer_u,
              peer_v, final_g):
    b = x.shape[0]
    cond = jax.nn.silu(c)
    for i in range(DEPTH):
        mod = (cond @ w_mod[i] + b_mod[i]).reshape(b, N_MOD, D_MODEL)[:, :, None, :]
        shift1, scale1, gate1 = mod[:, 0], mod[:, 1], mod[:, 2]
        shift2, scale2, gate2 = mod[:, 3], mod[:, 4], mod[:, 5]

        h = _rmsnorm(x, norm1_g[i]) * (1.0 + scale1) + shift1
        proj = h @ w_in[i] + b_in[i]
        out_a = _spatial_gating(proj[..., :2 * WIDTH_A], gm_ln_g[i], gm_ln_b[i], gm_ws[i], gm_bs[i])
        out_b = _s5(proj[..., 2 * WIDTH_A:], ssm_lam_re[i], ssm_lam_im[i], ssm_log_dt[i],
                    ssm_b_re[i], ssm_b_im[i], ssm_c_re[i], ssm_c_im[i], ssm_d[i], glu_w[i], glu_b[i])
        mixed = jnp.concatenate([_rmsnorm(out_a, branch_g[i, :WIDTH_A]),
                                 _rmsnorm(out_b, branch_g[i, WIDTH_A:])], axis=-1)
        x = x + gate1 * (mixed @ w_out[i])

        h = _rmsnorm(x, norm2_g[i]) * (1.0 + scale2) + shift2
        x = x + gate2 * _peer(h, peer_wq[i], peer_keys[i], peer_u[i], peer_v[i])
    return _rmsnorm(x, final_g)
```

```python
import functools
import math

import jax
import jax.numpy as jnp
from jax import lax
from jax.experimental import pallas as pl
from jax.experimental.pallas import tpu as pltpu

F32 = jnp.float32
BF16 = jnp.bfloat16

D_MODEL = 2048
SEQ = 16384
DEPTH = 2
CHUNK = 64
WIDTH_A = 1024
WIDTH_B = 1024
IN_WIDTH = 3072
GMLP_BLOCK = 128
GMLP_HEADS = 8
SSM_GROUP = 16
SSM_GROUPS = 64
SSM_STATE = 64
PEER_HEADS = 8
PEER_KEYS = 128
PEER_EXPERTS = PEER_KEYS * PEER_KEYS
PEER_TOPK = 16
N_MOD = 6
EPS = 1e-6

LANES = 128
N_CHUNKS = SEQ // CHUNK
CHUNK_W = CHUNK * SSM_GROUP
VMEM_LIMIT = 56 * 1024 * 1024

TM_MIX = 256
TQ_TOPK = 256
TG_BUILD = 64
TB_DENSE = 1024
EB_DENSE = 512
TM_FINAL = 512
TN_MOD = 1024

_NEG_INF = float("-inf")
_BIG = float(1 << 20)


def _cparams(sem):
    return pltpu.CompilerParams(dimension_semantics=sem, vmem_limit_bytes=VMEM_LIMIT)


def _rms(x, g):
    return x * lax.rsqrt(jnp.mean(x * x, axis=-1, keepdims=True) + EPS) * g


def _mod_kernel(c_ref, w_ref, b_ref, o_ref):
    c = c_ref[...]
    cond = c / (1.0 + jnp.exp(-c))
    w = w_ref[0]
    o_ref[0] = jnp.sum(w * cond, axis=0, keepdims=True) + b_ref[0]


def _mod_call(c, w_mod, b_mod):
    n = N_MOD * D_MODEL
    return pl.pallas_call(
        _mod_kernel,
        grid=(DEPTH, n // TN_MOD),
        in_specs=[
            pl.BlockSpec((D_MODEL, 1), lambda l, j: (0, 0)),
            pl.BlockSpec((1, D_MODEL, TN_MOD), lambda l, j: (l, 0, j)),
            pl.BlockSpec((1, 1, TN_MOD), lambda l, j: (l, 0, j)),
        ],
        out_specs=pl.BlockSpec((1, 1, TN_MOD), lambda l, j: (l, 0, j)),
        out_shape=jax.ShapeDtypeStruct((DEPTH, 1, n), F32),
        compiler_params=_cparams(("arbitrary", "arbitrary")),
        name="mod_gemv",
    )(c.reshape(D_MODEL, 1), w_mod, b_mod.reshape(DEPTH, 1, n))


def _mix_in_body(x, n1g, sc1, sh1, win_ref, bin_ref, lng, lnb, wm_ref, bs_ref, bga,
                 ma_ref, xb_ref):
    h = (_rms(x, n1g) * (1.0 + sc1) + sh1).astype(BF16)
    proj = jnp.dot(h, win_ref[...], preferred_element_type=F32) + bin_ref[...]
    z = jax.nn.gelu(proj[:, :2 * WIDTH_A])
    u = z[:, :WIDTH_A]
    v = z[:, WIDTH_A:]
    mu = jnp.mean(v, axis=-1, keepdims=True)
    vc = v - mu
    var = jnp.mean(vc * vc, axis=-1, keepdims=True)
    vn = (vc * lax.rsqrt(var + EPS) * lng + lnb).astype(BF16)
    tm = x.shape[0]
    rows = []
    for n in range(tm // GMLP_BLOCK):
        vb = vn[n * GMLP_BLOCK:(n + 1) * GMLP_BLOCK, :]
        cols = []
        for hh in range(GMLP_HEADS):
            cols.append(jnp.dot(wm_ref[hh], vb[:, hh * LANES:(hh + 1) * LANES],
                                preferred_element_type=F32))
        rows.append(jnp.concatenate(cols, axis=1) + bs_ref[...])
    s = jnp.concatenate(rows, axis=0)
    out_a = u * s
    ma_ref[...] = _rms(out_a, bga).astype(BF16)
    xb_ref[...] = proj[:, 2 * WIDTH_A:]


def _mix_in_kernel0(x_ref, n1g, sc1, sh1, win_ref, bin_ref, lng, lnb, wm_ref, bs_ref, bga,
                    ma_ref, xb_ref):
    _mix_in_body(x_ref[...], n1g[...], sc1[...], sh1[...], win_ref, bin_ref, lng[...], lnb[...],
                 wm_ref, bs_ref, bga[...], ma_ref, xb_ref)


def _mix_in_kernel1(x_ref, p_ref, g2_ref, n1g, sc1, sh1, win_ref, bin_ref, lng, lnb, wm_ref,
                    bs_ref, bga, xo_ref, ma_ref, xb_ref):
    x = x_ref[...] + g2_ref[...] * p_ref[...]
    xo_ref[...] = x
    _mix_in_body(x, n1g[...], sc1[...], sh1[...], win_ref, bin_ref, lng[...], lnb[...],
                 wm_ref, bs_ref, bga[...], ma_ref, xb_ref)


def _row_spec(width):
    return pl.BlockSpec((1, width), lambda i: (0, 0))


def _tile_spec(tm, width):
    return pl.BlockSpec((tm, width), lambda i: (i, 0))


def _full_spec(shape):
    nd = len(shape)
    return pl.BlockSpec(shape, lambda i: (0,) * nd)


def _mix_in_call(x, res, n1g, sc1, sh1, w_in, b_in, lng, lnb, wm, bs_full, bga):
    tm = TM_MIX
    common_specs = [
        _row_spec(D_MODEL), _row_spec(D_MODEL), _row_spec(D_MODEL),
        _full_spec((D_MODEL, IN_WIDTH)), _row_spec(IN_WIDTH),
        _row_spec(WIDTH_A), _row_spec(WIDTH_A),
        _full_spec((GMLP_HEADS, GMLP_BLOCK, GMLP_BLOCK)),
        _full_spec((GMLP_BLOCK, WIDTH_A)),
        _row_spec(WIDTH_A),
    ]
    common_args = (n1g, sc1, sh1, w_in, b_in, lng, lnb, wm, bs_full, bga)
    out_specs = [_tile_spec(tm, WIDTH_A), _tile_spec(tm, WIDTH_B)]
    out_shape = [jax.ShapeDtypeStruct((SEQ, WIDTH_A), BF16),
                 jax.ShapeDtypeStruct((SEQ, WIDTH_B), F32)]
    if res is None:
        ma, xb = pl.pallas_call(
            _mix_in_kernel0, grid=(SEQ // tm,),
            in_specs=[_tile_spec(tm, D_MODEL)] + common_specs,
            out_specs=out_specs, out_shape=out_shape,
            compiler_params=_cparams(("arbitrary",)), name="mix_in",
        )(x, *common_args)
        return x, ma, xb
    peer, g2 = res
    xo, ma, xb = pl.pallas_call(
        _mix_in_kernel1, grid=(SEQ // tm,),
        in_specs=[_tile_spec(tm, D_MODEL), _tile_spec(tm, D_MODEL), _row_spec(D_MODEL)]
        + common_specs,
        out_specs=[_tile_spec(tm, D_MODEL)] + out_specs,
        out_shape=[jax.ShapeDtypeStruct((SEQ, D_MODEL), F32)] + out_shape,
        compiler_params=_cparams(("arbitrary",)), name="mix_in_res",
    )(x, peer, g2, *common_args)
    return xo, ma, xb


def _s5_local_kernel(x_ref, w_ref, o_ref):
    o_ref[...] = jnp.dot(x_ref[0], w_ref[0], preferred_element_type=F32)


def _s5_scan_kernel(l_ref, ca_ref, cb_ref, h_ref):
    ca = ca_ref[...]
    cb = cb_ref[...]
    n_half = ca.shape[1] // LANES

    def step(k, s):
        h_ref[k] = s
        pieces = []
        for j in range(0, n_half, 2):
            pieces.append(s[:, (j + 1) * LANES:(j + 2) * LANES])
            pieces.append(s[:, j * LANES:(j + 1) * LANES])
        sw = jnp.concatenate(pieces, axis=1)
        return ca * s + cb * sw + l_ref[k]

    lax.fori_loop(0, N_CHUNKS, step, jnp.zeros(ca.shape, F32))


def _s5_out_kernel(x_ref, m_ref, h_ref, wo_ref, y_ref):
    y = jnp.dot(x_ref[0], m_ref[0], preferred_element_type=F32)
    h = h_ref[...]
    h_hi = h.astype(BF16)
    h_lo = (h - h_hi.astype(F32)).astype(BF16)
    y += jnp.dot(h_hi, wo_ref[0], preferred_element_type=F32)
    y += jnp.dot(h_lo, wo_ref[0], preferred_element_type=F32)
    y_ref[0] = y


def _s5_call(xb, prep):
    m_toep, w_state, w_out, ca, cb = prep
    g = SSM_GROUPS
    xg = xb.reshape(N_CHUNKS, CHUNK, g, SSM_GROUP).transpose(2, 0, 1, 3)
    xg = xg.reshape(g, N_CHUNKS, CHUNK_W).astype(BF16)
    local = pl.pallas_call(
        _s5_local_kernel, grid=(g,),
        in_specs=[pl.BlockSpec((1, N_CHUNKS, CHUNK_W), lambda i: (i, 0, 0)),
                  pl.BlockSpec((1, CHUNK_W, 4 * SSM_STATE), lambda i: (i, 0, 0))],
        out_specs=pl.BlockSpec((N_CHUNKS, 4 * SSM_STATE), lambda i: (0, i)),
        out_shape=jax.ShapeDtypeStruct((N_CHUNKS, g * 4 * SSM_STATE), F32),
        compiler_params=_cparams(("arbitrary",)), name="s5_local",
    )(xg, w_state)
    rows = 8
    width = g * 4 * SSM_STATE // rows
    hstart = pl.pallas_call(
        _s5_scan_kernel,
        out_shape=jax.ShapeDtypeStruct((N_CHUNKS, rows, width), F32),
        compiler_params=pltpu.CompilerParams(vmem_limit_bytes=VMEM_LIMIT), name="s5_scan",
    )(local.reshape(N_CHUNKS, rows, width), ca, cb)
    hstart = hstart.reshape(N_CHUNKS, g * 4 * SSM_STATE)
    yg = pl.pallas_call(
        _s5_out_kernel, grid=(g,),
        in_specs=[pl.BlockSpec((1, N_CHUNKS, CHUNK_W), lambda i: (i, 0, 0)),
                  pl.BlockSpec((1, CHUNK_W, CHUNK_W), lambda i: (i, 0, 0)),
                  pl.BlockSpec((N_CHUNKS, 2 * SSM_STATE), lambda i: (0, 2 * i)),
                  pl.BlockSpec((1, 2 * SSM_STATE, CHUNK_W), lambda i: (i, 0, 0))],
        out_specs=pl.BlockSpec((1, N_CHUNKS, CHUNK_W), lambda i: (i, 0, 0)),
        out_shape=jax.ShapeDtypeStruct((g, N_CHUNKS, CHUNK_W), F32),
        compiler_params=_cparams(("arbitrary",)), name="s5_out",
    )(xg, m_toep, hstart, w_out)
    y = yg.reshape(g, N_CHUNKS, CHUNK, SSM_GROUP).transpose(1, 2, 0, 3)
    return y.reshape(SEQ, WIDTH_B)


def _s5_prep(lam_re, lam_im, log_dt, b_re, b_im, c_re, c_im):
    hp = lax.Precision.HIGHEST
    lam = lax.complex(lam_re, lam_im)
    dt = jnp.exp(log_dt)[:, None]
    a_bar = jnp.exp(lam * dt)
    b_bar = ((a_bar - 1.0) / lam)[..., None] * lax.complex(b_re, b_im)
    steps = jnp.arange(0, CHUNK + 1, dtype=F32)
    apow = jnp.exp(lam[None] * (dt[None] * steps[:, None, None]))
    e = apow[:CHUNK, :, :, None] * b_bar[None]
    k_tap = (jnp.einsum("ghp,tgpk->tghk", c_re, jnp.real(e), precision=hp)
             - jnp.einsum("ghp,tgpk->tghk", c_im, jnp.imag(e), precision=hp))
    pos = jnp.arange(CHUNK)
    lag = pos[None, :] - pos[:, None]
    toep = jnp.where((lag >= 0)[:, :, None, None, None],
                     k_tap[jnp.clip(lag, 0, CHUNK - 1)], 0.0)
    m_toep = toep.transpose(2, 0, 4, 1, 3).reshape(SSM_GROUPS, CHUNK_W, CHUNK_W).astype(BF16)
    w = apow[CHUNK - 1::-1][:CHUNK, :, :, None] * b_bar[None]
    w = w.transpose(1, 0, 3, 2).reshape(SSM_GROUPS, CHUNK_W, SSM_STATE)
    w_state = jnp.concatenate([jnp.real(w), jnp.imag(w), jnp.imag(w), jnp.real(w)],
                              axis=-1).astype(BF16)
    cmat = lax.complex(c_re, c_im)
    ca_ = cmat[None] * apow[1:CHUNK + 1, :, None, :]
    ca_ = ca_.transpose(1, 3, 0, 2).reshape(SSM_GROUPS, SSM_STATE, CHUNK_W)
    w_out = jnp.concatenate([jnp.real(ca_), -jnp.imag(ca_)], axis=1).astype(BF16)
    a_chunk = apow[CHUNK]
    ar, ai = jnp.real(a_chunk), jnp.imag(a_chunk)
    ca = jnp.concatenate([ar, ar, ar, ar], axis=-1).reshape(8, -1)
    cb = jnp.concatenate([-ai, ai, ai, -ai], axis=-1).reshape(8, -1)
    return m_toep, w_state, w_out, ca, cb


def _mix_out_kernel(x_ref, ma_ref, xb_ref, ys_ref, d_ref, gw_ref, gb_ref, bgb_ref, wo_ref,
                    g1_ref, n2g_ref, sc2_ref, sh2_ref, xo_ref, h2_ref):
    y = ys_ref[...] + d_ref[...] * xb_ref[...]
    g = jax.nn.gelu(y)
    gate = jax.nn.sigmoid(jnp.dot(g.astype(BF16), gw_ref[...], preferred_element_type=F32)
                          + gb_ref[...])
    mb = _rms(g * gate, bgb_ref[...]).astype(BF16)
    mo = jnp.dot(ma_ref[...], wo_ref[:WIDTH_A, :], preferred_element_type=F32)
    mo += jnp.dot(mb, wo_ref[WIDTH_A:, :], preferred_element_type=F32)
    x = x_ref[...] + g1_ref[...] * mo
    xo_ref[...] = x
    h2_ref[...] = (_rms(x, n2g_ref[...]) * (1.0 + sc2_ref[...]) + sh2_ref[...]).astype(BF16)


def _mix_out_call(x, ma, xb, ys, d, glu_w, glu_b, bgb, w_out, g1, n2g, sc2, sh2):
    tm = TM_MIX
    return pl.pallas_call(
        _mix_out_kernel, grid=(SEQ // tm,),
        in_specs=[_tile_spec(tm, D_MODEL), _tile_spec(tm, WIDTH_A), _tile_spec(tm, WIDTH_B),
                  _tile_spec(tm, WIDTH_B), _row_spec(WIDTH_B),
                  _full_spec((WIDTH_B, WIDTH_B)), _row_spec(WIDTH_B), _row_spec(WIDTH_B),
                  _full_spec((D_MODEL, D_MODEL)),
                  _row_spec(D_MODEL), _row_spec(D_MODEL), _row_spec(D_MODEL), _row_spec(D_MODEL)],
        out_specs=[_tile_spec(tm, D_MODEL), _tile_spec(tm, D_MODEL)],
        out_shape=[jax.ShapeDtypeStruct((SEQ, D_MODEL), F32),
                   jax.ShapeDtypeStruct((SEQ, D_MODEL), BF16)],
        compiler_params=_cparams(("arbitrary",)), name="mix_out",
    )(x, ma, xb, ys, d, glu_w, glu_b, bgb, w_out, g1, n2g, sc2, sh2)


def _topk_rows(x, k):
    n, w = x.shape
    riota = lax.broadcasted_iota(jnp.int32, (n, w), 0).astype(F32)
    kiota = lax.broadcasted_iota(jnp.int32, (k, w), 0)

    def body(it, carry):
        x, vals, idxs = carry
        m = jnp.max(x, axis=0, keepdims=True)
        am = jnp.min(jnp.where(x == m, riota, float(n)), axis=0, keepdims=True)
        x = jnp.where(riota == am, _NEG_INF, x)
        vals = jnp.where(kiota == it, m, vals)
        idxs = jnp.where(kiota == it, am, idxs)
        return x, vals, idxs

    _, vals, idxs = lax.fori_loop(
        0, k, body, (x, jnp.zeros((k, w), F32), jnp.zeros((k, w), F32)))
    return vals, idxs


def _pair_candidates(v1, i1, v2, i2):
    k, w = v1.shape
    r = lax.broadcasted_iota(jnp.int32, (k, w), 0)
    rf = r.astype(F32)
    cands, flats, ids = [], [], []

    def add(valid, s, flat, e):
        cands.append(jnp.where(valid, s, _NEG_INF))
        flats.append(jnp.where(valid, flat, _BIG))
        ids.append(e)

    for a in range(4):
        add(r < PEER_TOPK // (a + 1), v1[a:a + 1, :] + v2, float(a * PEER_TOPK) + rf,
            i1[a:a + 1, :] * float(PEER_KEYS) + i2)
    for b in range(3):
        hi = PEER_TOPK // (b + 1)
        valid = jnp.logical_and(r >= 4, r < hi)
        add(valid, v1 + v2[b:b + 1, :], rf * float(PEER_TOPK) + float(b),
            i1 * float(PEER_KEYS) + i2[b:b + 1, :])
    return (jnp.concatenate(cands, axis=0), jnp.concatenate(flats, axis=0),
            jnp.concatenate(ids, axis=0))


def _topk_pairs(cand, flat, ids, k):
    n, w = cand.shape
    kiota = lax.broadcasted_iota(jnp.int32, (k, w), 0)

    def body(it, carry):
        cand, vals, sel_ids = carry
        m = jnp.max(cand, axis=0, keepdims=True)
        fm = jnp.min(jnp.where(cand == m, flat, _BIG), axis=0, keepdims=True)
        sel = flat == fm
        e = jnp.max(jnp.where(sel, ids, -1.0), axis=0, keepdims=True)
        cand = jnp.where(sel, _NEG_INF, cand)
        vals = jnp.where(kiota == it, m, vals)
        sel_ids = jnp.where(kiota == it, e, sel_ids)
        return cand, vals, sel_ids

    _, vals, sel_ids = lax.fori_loop(
        0, k, body, (cand, jnp.zeros((k, w), F32), jnp.zeros((k, w), F32)))
    return vals, sel_ids


def _peer_topk_kernel(h2_ref, wqt_ref, keys_ref, e_ref, g_ref, qt_ref):
    qt_ref[...] = lax.dot_general(wqt_ref[...], h2_ref[...], (((1,), (1,)), ((), ())),
                                  preferred_element_type=F32).astype(BF16)
    n_tiles = h2_ref.shape[0] // LANES

    def head(hh, _):
        sides = []
        for s in range(2):
            row0 = pl.multiple_of(hh * (2 * PEER_KEYS) + s * PEER_KEYS, PEER_KEYS)
            sides.append(jnp.dot(keys_ref[hh * 2 + s], qt_ref[pl.ds(row0, PEER_KEYS), :],
                                 preferred_element_type=F32))
        out0 = pl.multiple_of(hh * PEER_TOPK, PEER_TOPK)
        for t in range(n_tiles):
            lanes = slice(t * LANES, (t + 1) * LANES)
            v1, i1 = _topk_rows(sides[0][:, lanes], PEER_TOPK)
            v2, i2 = _topk_rows(sides[1][:, lanes], PEER_TOPK)
            best, ids = _topk_pairs(*_pair_candidates(v1, i1, v2, i2), PEER_TOPK)
            ex = jnp.exp(best - jnp.max(best, axis=0, keepdims=True))
            gate = ex / jnp.sum(ex, axis=0, keepdims=True)
            e_ref[pl.ds(out0, PEER_TOPK), lanes] = ids.astype(jnp.int32)
            g_ref[pl.ds(out0, PEER_TOPK), lanes] = gate
        return 0

    lax.fori_loop(0, PEER_HEADS, head, 0)


def _peer_topk_call(h2, wq_t, keys):
    tq = TQ_TOPK
    n_slots = PEER_HEADS * PEER_TOPK
    return pl.pallas_call(
        _peer_topk_kernel, grid=(SEQ // tq,),
        in_specs=[_tile_spec(tq, D_MODEL), _full_spec((D_MODEL, D_MODEL)),
                  _full_spec((2 * PEER_HEADS, PEER_KEYS, PEER_KEYS))],
        out_specs=[pl.BlockSpec((n_slots, tq), lambda i: (0, i)),
                   pl.BlockSpec((n_slots, tq), lambda i: (0, i))],
        out_shape=[jax.ShapeDtypeStruct((n_slots, SEQ), jnp.int32),
                   jax.ShapeDtypeStruct((n_slots, SEQ), F32)],
        scratch_shapes=[pltpu.VMEM((D_MODEL, tq), BF16)],
        compiler_params=_cparams(("arbitrary",)), name="peer_topk",
    )(h2, wq_t, keys)


def _peer_gate_kernel(e_ref, g_ref, o_ref):
    n_slots = e_ref.shape[1]
    kio = lax.broadcasted_iota(jnp.int32, (PEER_KEYS, n_slots), 0)

    def token(t, _):
        e = e_ref[pl.ds(t, 1), :]
        gate = g_ref[pl.ds(t, 1), :]
        i1 = e >> 7
        i2 = e & (PEER_KEYS - 1)
        oh1 = jnp.where(i1 == kio, 1.0, 0.0).astype(BF16)
        w2 = jnp.where(i2 == kio, gate, 0.0).astype(BF16)
        o_ref[t] = lax.dot_general(oh1, w2, (((1,), (1,)), ((), ())),
                                   preferred_element_type=F32).astype(BF16)
        return 0

    lax.fori_loop(0, e_ref.shape[0], token, 0)


def _peer_gate_call(e_rows, g_rows):
    tg = TG_BUILD
    n_slots = PEER_HEADS * PEER_TOPK
    return pl.pallas_call(
        _peer_gate_kernel, grid=(SEQ // tg,),
        in_specs=[_tile_spec(tg, n_slots), _tile_spec(tg, n_slots)],
        out_specs=pl.BlockSpec((tg, PEER_KEYS, PEER_KEYS), lambda i: (i, 0, 0)),
        out_shape=jax.ShapeDtypeStruct((SEQ, PEER_KEYS, PEER_KEYS), BF16),
        compiler_params=_cparams(("arbitrary",)), name="peer_gate",
    )(e_rows, g_rows)


def _peer_dense_kernel(h2_ref, u_ref, v_ref, g_ref, o_ref):
    j = pl.program_id(1)
    a = lax.dot_general(h2_ref[...], u_ref[...], (((1,), (1,)), ((), ())),
                        preferred_element_type=F32)
    p = (g_ref[...].astype(F32) * jax.nn.gelu(a)).astype(BF16)
    contrib = jnp.dot(p, v_ref[...], preferred_element_type=F32)

    @pl.when(j == 0)
    def _():
        o_ref[...] = contrib

    @pl.when(j > 0)
    def _():
        o_ref[...] += contrib


def _peer_dense_call(h2, u_tab, v_tab, gmat):
    tb, eb = TB_DENSE, EB_DENSE
    return pl.pallas_call(
        _peer_dense_kernel, grid=(SEQ // tb, PEER_EXPERTS // eb),
        in_specs=[pl.BlockSpec((tb, D_MODEL), lambda i, j: (i, 0)),
                  pl.BlockSpec((eb, D_MODEL), lambda i, j: (j, 0)),
                  pl.BlockSpec((eb, D_MODEL), lambda i, j: (j, 0)),
                  pl.BlockSpec((tb, eb), lambda i, j: (i, j))],
        out_specs=pl.BlockSpec((tb, D_MODEL), lambda i, j: (i, 0)),
        out_shape=jax.ShapeDtypeStruct((SEQ, D_MODEL), F32),
        compiler_params=_cparams(("arbitrary", "arbitrary")), name="peer_dense",
    )(h2, u_tab, v_tab, gmat)


def _final_kernel(x_ref, p_ref, g2_ref, fg_ref, o_ref):
    x = x_ref[...] + g2_ref[...] * p_ref[...]
    o_ref[...] = _rms(x, fg_ref[...])


def _final_call(x, peer, g2, fg):
    tm = TM_FINAL
    return pl.pallas_call(
        _final_kernel, grid=(SEQ // tm,),
        in_specs=[_tile_spec(tm, D_MODEL), _tile_spec(tm, D_MODEL), _row_spec(D_MODEL),
                  _row_spec(D_MODEL)],
        out_specs=_tile_spec(tm, D_MODEL),
        out_shape=jax.ShapeDtypeStruct((SEQ, D_MODEL), F32),
        compiler_params=_cparams(("arbitrary",)), name="final_norm",
    )(x, peer, g2, fg)


def _chunk_causal_mask(n):
    blk = jnp.arange(n) // CHUNK
    return blk[:, None] >= blk[None, :]


def kernel(x, c, w_mod, b_mod, norm1_g, w_in, b_in, gm_ln_g, gm_ln_b, gm_ws, gm_bs, ssm_lam_re, ssm_lam_im, ssm_log_dt, ssm_b_re, ssm_b_im, ssm_c_re, ssm_c_im, ssm_d, glu_w, glu_b, branch_g, w_out, norm2_g, peer_wq, peer_keys, peer_u, peer_v, final_g):
    assert x.shape == (1, SEQ, D_MODEL)
    xt = x.reshape(SEQ, D_MODEL)
    mod = _mod_call(c, w_mod, b_mod).reshape(DEPTH, N_MOD, 1, D_MODEL)
    row = lambda v: v.reshape(1, -1)
    res = None
    for i in range(DEPTH):
        sh1, sc1, g1, sh2, sc2, g2 = (mod[i, k] for k in range(N_MOD))
        wm = jnp.where(_chunk_causal_mask(GMLP_BLOCK)[None], gm_ws[i], 0.0).astype(BF16)
        bs_full = jnp.repeat(gm_bs[i].T, LANES, axis=1)
        xt, ma, xb = _mix_in_call(
            xt, res, row(norm1_g[i]), sc1, sh1, w_in[i].astype(BF16), row(b_in[i]),
            row(gm_ln_g[i]), row(gm_ln_b[i]), wm, bs_full, row(branch_g[i, :WIDTH_A]))
        prep = _s5_prep(ssm_lam_re[i], ssm_lam_im[i], ssm_log_dt[i], ssm_b_re[i], ssm_b_im[i],
                        ssm_c_re[i], ssm_c_im[i])
        ys = _s5_call(xb, prep)
        xt, h2 = _mix_out_call(
            xt, ma, xb, ys, row(ssm_d[i]), glu_w[i].astype(BF16), row(glu_b[i]),
            row(branch_g[i, WIDTH_A:]), w_out[i].astype(BF16), g1, row(norm2_g[i]), sc2, sh2)
        keys = peer_keys[i].reshape(2 * PEER_HEADS, PEER_KEYS, PEER_KEYS).astype(BF16)
        e_t, gate_t = _peer_topk_call(h2, peer_wq[i].T.astype(BF16), keys)
        gmat = _peer_gate_call(e_t.T, gate_t.T).reshape(SEQ, PEER_EXPERTS)
        peer = _peer_dense_call(h2, peer_u[i].astype(BF16), peer_v[i].astype(BF16), gmat)
        res = (peer, g2)
    out = _final_call(xt, res[0], res[1], row(final_g))
    return out.reshape(1, SEQ, D_MODEL)
```

```python
import functools
import math

import jax
import jax.numpy as jnp
from jax import lax
from jax.experimental import pallas as pl
from jax.experimental.pallas import tpu as pltpu

F32 = jnp.float32
BF16 = jnp.bfloat16

D_MODEL = 2048
SEQ = 16384
DEPTH = 2
CHUNK = 64
WIDTH_A = 1024
WIDTH_B = 1024
IN_WIDTH = 3072
GMLP_BLOCK = 128
GMLP_HEADS = 8
SSM_GROUP = 16
SSM_GROUPS = 64
SSM_STATE = 64
PEER_HEADS = 8
PEER_KEYS = 128
PEER_EXPERTS = PEER_KEYS * PEER_KEYS
PEER_TOPK = 16
N_MOD = 6
EPS = 1e-6

LANES = 128
N_CHUNKS = SEQ // CHUNK
CHUNK_W = CHUNK * SSM_GROUP
VMEM_LIMIT = 56 * 1024 * 1024

TM_MIX = 256
TQ_TOPK = 256
TG_BUILD = 64
GATE_UNROLL = 8
TB_DENSE = 1024
EB_DENSE = 512
KEY1_PER_BLOCK = EB_DENSE // PEER_KEYS
F32_SUBLANES = 8
G_BLOCKS_PER_FETCH = F32_SUBLANES // KEY1_PER_BLOCK
TM_FINAL = 512
TN_MOD = 1024

_NEG_INF = float("-inf")
_BIG = float(1 << 20)


def _cparams(sem):
    return pltpu.CompilerParams(dimension_semantics=sem, vmem_limit_bytes=VMEM_LIMIT)


def _rms(x, g):
    return x * lax.rsqrt(jnp.mean(x * x, axis=-1, keepdims=True) + EPS) * g


def _mod_kernel(c_ref, w_ref, b_ref, o_ref):
    c = c_ref[...]
    cond = c / (1.0 + jnp.exp(-c))
    w = w_ref[0]
    o_ref[0] = jnp.sum(w * cond, axis=0, keepdims=True) + b_ref[0]


def _mod_call(c, w_mod, b_mod):
    n = N_MOD * D_MODEL
    return pl.pallas_call(
        _mod_kernel,
        grid=(DEPTH, n // TN_MOD),
        in_specs=[
            pl.BlockSpec((D_MODEL, 1), lambda l, j: (0, 0)),
            pl.BlockSpec((1, D_MODEL, TN_MOD), lambda l, j: (l, 0, j)),
            pl.BlockSpec((1, 1, TN_MOD), lambda l, j: (l, 0, j)),
        ],
        out_specs=pl.BlockSpec((1, 1, TN_MOD), lambda l, j: (l, 0, j)),
        out_shape=jax.ShapeDtypeStruct((DEPTH, 1, n), F32),
        compiler_params=_cparams(("arbitrary", "arbitrary")),
        name="mod_gemv",
    )(c.reshape(D_MODEL, 1), w_mod, b_mod.reshape(DEPTH, 1, n))


def _mix_in_body(x, n1g, sc1, sh1, win_ref, bin_ref, lng, lnb, wm_ref, bs_ref, bga,
                 ma_ref, xb_ref):
    h = (_rms(x, n1g) * (1.0 + sc1) + sh1).astype(BF16)
    proj = jnp.dot(h, win_ref[...], preferred_element_type=F32) + bin_ref[...]
    z = jax.nn.gelu(proj[:, :2 * WIDTH_A])
    u = z[:, :WIDTH_A]
    v = z[:, WIDTH_A:]
    mu = jnp.mean(v, axis=-1, keepdims=True)
    vc = v - mu
    var = jnp.mean(vc * vc, axis=-1, keepdims=True)
    vn = (vc * lax.rsqrt(var + EPS) * lng + lnb).astype(BF16)
    tm = x.shape[0]
    rows = []
    for n in range(tm // GMLP_BLOCK):
        vb = vn[n * GMLP_BLOCK:(n + 1) * GMLP_BLOCK, :]
        cols = []
        for hh in range(GMLP_HEADS):
            cols.append(jnp.dot(wm_ref[hh], vb[:, hh * LANES:(hh + 1) * LANES],
                                preferred_element_type=F32))
        rows.append(jnp.concatenate(cols, axis=1) + bs_ref[...])
    s = jnp.concatenate(rows, axis=0)
    out_a = u * s
    ma_ref[...] = _rms(out_a, bga).astype(BF16)
    xb_ref[...] = proj[:, 2 * WIDTH_A:]


def _mix_in_kernel0(x_ref, n1g, sc1, sh1, win_ref, bin_ref, lng, lnb, wm_ref, bs_ref, bga,
                    ma_ref, xb_ref):
    _mix_in_body(x_ref[...], n1g[...], sc1[...], sh1[...], win_ref, bin_ref, lng[...], lnb[...],
                 wm_ref, bs_ref, bga[...], ma_ref, xb_ref)


def _mix_in_kernel1(x_ref, p_ref, g2_ref, n1g, sc1, sh1, win_ref, bin_ref, lng, lnb, wm_ref,
                    bs_ref, bga, xo_ref, ma_ref, xb_ref):
    x = x_ref[...] + g2_ref[...] * p_ref[...]
    xo_ref[...] = x
    _mix_in_body(x, n1g[...], sc1[...], sh1[...], win_ref, bin_ref, lng[...], lnb[...],
                 wm_ref, bs_ref, bga[...], ma_ref, xb_ref)


def _row_spec(width):
    return pl.BlockSpec((1, width), lambda i: (0, 0))


def _tile_spec(tm, width):
    return pl.BlockSpec((tm, width), lambda i: (i, 0))


def _full_spec(shape):
    nd = len(shape)
    return pl.BlockSpec(shape, lambda i: (0,) * nd)


def _mix_in_call(x, res, n1g, sc1, sh1, w_in, b_in, lng, lnb, wm, bs_full, bga):
    tm = TM_MIX
    common_specs = [
        _row_spec(D_MODEL), _row_spec(D_MODEL), _row_spec(D_MODEL),
        _full_spec((D_MODEL, IN_WIDTH)), _row_spec(IN_WIDTH),
        _row_spec(WIDTH_A), _row_spec(WIDTH_A),
        _full_spec((GMLP_HEADS, GMLP_BLOCK, GMLP_BLOCK)),
        _full_spec((GMLP_BLOCK, WIDTH_A)),
        _row_spec(WIDTH_A),
    ]
    common_args = (n1g, sc1, sh1, w_in, b_in, lng, lnb, wm, bs_full, bga)
    out_specs = [_tile_spec(tm, WIDTH_A), _tile_spec(tm, WIDTH_B)]
    out_shape = [jax.ShapeDtypeStruct((SEQ, WIDTH_A), BF16),
                 jax.ShapeDtypeStruct((SEQ, WIDTH_B), F32)]
    if res is None:
        ma, xb = pl.pallas_call(
            _mix_in_kernel0, grid=(SEQ // tm,),
            in_specs=[_tile_spec(tm, D_MODEL)] + common_specs,
            out_specs=out_specs, out_shape=out_shape,
            compiler_params=_cparams(("arbitrary",)), name="mix_in",
        )(x, *common_args)
        return x, ma, xb
    peer, g2 = res
    xo, ma, xb = pl.pallas_call(
        _mix_in_kernel1, grid=(SEQ // tm,),
        in_specs=[_tile_spec(tm, D_MODEL), _tile_spec(tm, D_MODEL), _row_spec(D_MODEL)]
        + common_specs,
        out_specs=[_tile_spec(tm, D_MODEL)] + out_specs,
        out_shape=[jax.ShapeDtypeStruct((SEQ, D_MODEL), F32)] + out_shape,
        compiler_params=_cparams(("arbitrary",)), name="mix_in_res",
    )(x, peer, g2, *common_args)
    return xo, ma, xb


def _s5_local_kernel(x_ref, w_ref, o_ref):
    o_ref[...] = jnp.dot(x_ref[0], w_ref[0], preferred_element_type=F32)


def _s5_scan_kernel(l_ref, ca_ref, cb_ref, h_ref):
    ca = ca_ref[...]
    cb = cb_ref[...]
    n_half = ca.shape[1] // LANES

    def step(k, s):
        h_ref[k] = s
        pieces = []
        for j in range(0, n_half, 2):
            pieces.append(s[:, (j + 1) * LANES:(j + 2) * LANES])
            pieces.append(s[:, j * LANES:(j + 1) * LANES])
        sw = jnp.concatenate(pieces, axis=1)
        return ca * s + cb * sw + l_ref[k]

    lax.fori_loop(0, N_CHUNKS, step, jnp.zeros(ca.shape, F32))


def _s5_out_kernel(x_ref, m_ref, h_ref, wo_ref, y_ref):
    y = jnp.dot(x_ref[0], m_ref[0], preferred_element_type=F32)
    h = h_ref[...]
    h_hi = h.astype(BF16)
    h_lo = (h - h_hi.astype(F32)).astype(BF16)
    y += jnp.dot(h_hi, wo_ref[0], preferred_element_type=F32)
    y += jnp.dot(h_lo, wo_ref[0], preferred_element_type=F32)
    y_ref[0] = y


def _s5_call(xb, prep):
    m_toep, w_state, w_out, ca, cb = prep
    g = SSM_GROUPS
    xg = xb.reshape(N_CHUNKS, CHUNK, g, SSM_GROUP).transpose(2, 0, 1, 3)
    xg = xg.reshape(g, N_CHUNKS, CHUNK_W).astype(BF16)
    local = pl.pallas_call(
        _s5_local_kernel, grid=(g,),
        in_specs=[pl.BlockSpec((1, N_CHUNKS, CHUNK_W), lambda i: (i, 0, 0)),
                  pl.BlockSpec((1, CHUNK_W, 4 * SSM_STATE), lambda i: (i, 0, 0))],
        out_specs=pl.BlockSpec((N_CHUNKS, 4 * SSM_STATE), lambda i: (0, i)),
        out_shape=jax.ShapeDtypeStruct((N_CHUNKS, g * 4 * SSM_STATE), F32),
        compiler_params=_cparams(("arbitrary",)), name="s5_local",
    )(xg, w_state)
    rows = 8
    width = g * 4 * SSM_STATE // rows
    hstart = pl.pallas_call(
        _s5_scan_kernel,
        out_shape=jax.ShapeDtypeStruct((N_CHUNKS, rows, width), F32),
        compiler_params=pltpu.CompilerParams(vmem_limit_bytes=VMEM_LIMIT), name="s5_scan",
    )(local.reshape(N_CHUNKS, rows, width), ca, cb)
    hstart = hstart.reshape(N_CHUNKS, g * 4 * SSM_STATE)
    yg = pl.pallas_call(
        _s5_out_kernel, grid=(g,),
        in_specs=[pl.BlockSpec((1, N_CHUNKS, CHUNK_W), lambda i: (i, 0, 0)),
                  pl.BlockSpec((1, CHUNK_W, CHUNK_W), lambda i: (i, 0, 0)),
                  pl.BlockSpec((N_CHUNKS, 2 * SSM_STATE), lambda i: (0, 2 * i)),
                  pl.BlockSpec((1, 2 * SSM_STATE, CHUNK_W), lambda i: (i, 0, 0))],
        out_specs=pl.BlockSpec((1, N_CHUNKS, CHUNK_W), lambda i: (i, 0, 0)),
        out_shape=jax.ShapeDtypeStruct((g, N_CHUNKS, CHUNK_W), F32),
        compiler_params=_cparams(("arbitrary",)), name="s5_out",
    )(xg, m_toep, hstart, w_out)
    y = yg.reshape(g, N_CHUNKS, CHUNK, SSM_GROUP).transpose(1, 2, 0, 3)
    return y.reshape(SEQ, WIDTH_B)


def _s5_prep(lam_re, lam_im, log_dt, b_re, b_im, c_re, c_im):
    hp = lax.Precision.HIGHEST
    cmul = lambda ar, ai, br, bi: (ar * br - ai * bi, ar * bi + ai * br)
    dt = jnp.exp(log_dt)[:, None]
    steps = jnp.arange(0, CHUNK + 1, dtype=F32)[:, None, None]
    mag = jnp.exp(lam_re[None] * (dt[None] * steps))
    ang = lam_im[None] * (dt[None] * steps)
    pw_r, pw_i = mag * jnp.cos(ang), mag * jnp.sin(ang)
    n_r, n_i = pw_r[1] - 1.0, pw_i[1]
    den = lam_re * lam_re + lam_im * lam_im
    q_r = (n_r * lam_re + n_i * lam_im) / den
    q_i = (n_i * lam_re - n_r * lam_im) / den
    bb_r, bb_i = cmul(q_r[..., None], q_i[..., None], b_re, b_im)
    e_r, e_i = cmul(pw_r[:CHUNK, :, :, None], pw_i[:CHUNK, :, :, None], bb_r[None], bb_i[None])
    k_tap = (jnp.einsum("ghp,tgpk->tghk", c_re, e_r, precision=hp)
             - jnp.einsum("ghp,tgpk->tghk", c_im, e_i, precision=hp))
    krow = k_tap.transpose(1, 3, 0, 2).reshape(SSM_GROUPS, SSM_GROUP, CHUNK_W).astype(BF16)
    kpad = jnp.concatenate([jnp.zeros_like(krow), krow], axis=-1)
    m_toep = jnp.stack([kpad[:, :, CHUNK_W - SSM_GROUP * c:2 * CHUNK_W - SSM_GROUP * c]
                        for c in range(CHUNK)], axis=1).reshape(SSM_GROUPS, CHUNK_W, CHUNK_W)
    rev_r, rev_i = pw_r[CHUNK - 1::-1][:CHUNK], pw_i[CHUNK - 1::-1][:CHUNK]
    w_r, w_i = cmul(rev_r[..., None], rev_i[..., None], bb_r[None], bb_i[None])
    lay = lambda w: w.transpose(1, 0, 3, 2).reshape(SSM_GROUPS, CHUNK_W, SSM_STATE)
    w_r, w_i = lay(w_r), lay(w_i)
    w_state = jnp.concatenate([w_r, w_i, w_i, w_r], axis=-1).astype(BF16)
    ca_r, ca_i = cmul(c_re[None], c_im[None], pw_r[1:CHUNK + 1, :, None, :],
                      pw_i[1:CHUNK + 1, :, None, :])
    lay = lambda w: w.transpose(1, 3, 0, 2).reshape(SSM_GROUPS, SSM_STATE, CHUNK_W)
    w_out = jnp.concatenate([lay(ca_r), -lay(ca_i)], axis=1).astype(BF16)
    ar, ai = pw_r[CHUNK], pw_i[CHUNK]
    ca = jnp.concatenate([ar, ar, ar, ar], axis=-1).reshape(8, -1)
    cb = jnp.concatenate([-ai, ai, ai, -ai], axis=-1).reshape(8, -1)
    return m_toep, w_state, w_out, ca, cb


def _mix_out_kernel(x_ref, ma_ref, xb_ref, ys_ref, d_ref, gw_ref, gb_ref, bgb_ref, wo_ref,
                    g1_ref, n2g_ref, sc2_ref, sh2_ref, xo_ref, h2_ref):
    y = ys_ref[...] + d_ref[...] * xb_ref[...]
    g = jax.nn.gelu(y)
    gate = jax.nn.sigmoid(jnp.dot(g.astype(BF16), gw_ref[...], preferred_element_type=F32)
                          + gb_ref[...])
    mb = _rms(g * gate, bgb_ref[...]).astype(BF16)
    mo = jnp.dot(ma_ref[...], wo_ref[:WIDTH_A, :], preferred_element_type=F32)
    mo += jnp.dot(mb, wo_ref[WIDTH_A:, :], preferred_element_type=F32)
    x = x_ref[...] + g1_ref[...] * mo
    xo_ref[...] = x
    h2_ref[...] = (_rms(x, n2g_ref[...]) * (1.0 + sc2_ref[...]) + sh2_ref[...]).astype(BF16)


def _mix_out_call(x, ma, xb, ys, d, glu_w, glu_b, bgb, w_out, g1, n2g, sc2, sh2):
    tm = TM_MIX
    return pl.pallas_call(
        _mix_out_kernel, grid=(SEQ // tm,),
        in_specs=[_tile_spec(tm, D_MODEL), _tile_spec(tm, WIDTH_A), _tile_spec(tm, WIDTH_B),
                  _tile_spec(tm, WIDTH_B), _row_spec(WIDTH_B),
                  _full_spec((WIDTH_B, WIDTH_B)), _row_spec(WIDTH_B), _row_spec(WIDTH_B),
                  _full_spec((D_MODEL, D_MODEL)),
                  _row_spec(D_MODEL), _row_spec(D_MODEL), _row_spec(D_MODEL), _row_spec(D_MODEL)],
        out_specs=[_tile_spec(tm, D_MODEL), _tile_spec(tm, D_MODEL)],
        out_shape=[jax.ShapeDtypeStruct((SEQ, D_MODEL), F32),
                   jax.ShapeDtypeStruct((SEQ, D_MODEL), BF16)],
        compiler_params=_cparams(("arbitrary",)), name="mix_out",
    )(x, ma, xb, ys, d, glu_w, glu_b, bgb, w_out, g1, n2g, sc2, sh2)


def _topk_rows(xs, k):
    n, w = xs[0].shape
    riota = lax.broadcasted_iota(jnp.int32, (n, w), 0).astype(F32)
    kiota = lax.broadcasted_iota(jnp.int32, (k, w), 0)

    def body(it, carry):
        out = []
        for x, vals, idxs in carry:
            m = jnp.max(x, axis=0, keepdims=True)
            am = jnp.min(jnp.where(x == m, riota, float(n)), axis=0, keepdims=True)
            x = jnp.where(riota == am, _NEG_INF, x)
            vals = jnp.where(kiota == it, m, vals)
            idxs = jnp.where(kiota == it, am, idxs)
            out.append((x, vals, idxs))
        return tuple(out)

    zeros = jnp.zeros((k, w), F32)
    res = lax.fori_loop(0, k, body, tuple((x, zeros, zeros) for x in xs))
    return [(vals, idxs) for _, vals, idxs in res]


def _pair_candidates(v1, i1, v2, i2):
    k, w = v1.shape
    r = lax.broadcasted_iota(jnp.int32, (k, w), 0)
    rf = r.astype(F32)
    cands, flats, ids = [], [], []

    def add(valid, s, flat, e):
        cands.append(jnp.where(valid, s, _NEG_INF))
        flats.append(jnp.where(valid, flat, _BIG))
        ids.append(e)

    for a in range(4):
        add(r < PEER_TOPK // (a + 1), v1[a:a + 1, :] + v2, float(a * PEER_TOPK) + rf,
            i1[a:a + 1, :] * float(PEER_KEYS) + i2)
    for b in range(3):
        hi = PEER_TOPK // (b + 1)
        valid = jnp.logical_and(r >= 4, r < hi)
        add(valid, v1 + v2[b:b + 1, :], rf * float(PEER_TOPK) + float(b),
            i1 * float(PEER_KEYS) + i2[b:b + 1, :])
    return (jnp.concatenate(cands, axis=0), jnp.concatenate(flats, axis=0),
            jnp.concatenate(ids, axis=0))


def _topk_pairs(problems, k):
    w = problems[0][0].shape[1]
    kiota = lax.broadcasted_iota(jnp.int32, (k, w), 0)
    statics = [(flat, ids) for _, flat, ids in problems]

    def body(it, carry):
        out = []
        for (cand, vals, sel_ids), (flat, ids) in zip(carry, statics):
            m = jnp.max(cand, axis=0, keepdims=True)
            fm = jnp.min(jnp.where(cand == m, flat, _BIG), axis=0, keepdims=True)
            sel = flat == fm
            e = jnp.max(jnp.where(sel, ids, -1.0), axis=0, keepdims=True)
            cand = jnp.where(sel, _NEG_INF, cand)
            vals = jnp.where(kiota == it, m, vals)
            sel_ids = jnp.where(kiota == it, e, sel_ids)
            out.append((cand, vals, sel_ids))
        return tuple(out)

    zeros = jnp.zeros((k, w), F32)
    res = lax.fori_loop(0, k, body, tuple((cand, zeros, zeros) for cand, _, _ in problems))
    return [(vals, sel_ids) for _, vals, sel_ids in res]


def _peer_topk_kernel(h2_ref, wqt_ref, keys_ref, e_ref, g_ref, qt_ref):
    qt_ref[...] = lax.dot_general(wqt_ref[...], h2_ref[...], (((1,), (1,)), ((), ())),
                                  preferred_element_type=F32).astype(BF16)
    n_tiles = h2_ref.shape[0] // LANES

    def head(hh, _):
        sides = []
        for s in range(2):
            row0 = pl.multiple_of(hh * (2 * PEER_KEYS) + s * PEER_KEYS, PEER_KEYS)
            sides.append(jnp.dot(keys_ref[hh * 2 + s], qt_ref[pl.ds(row0, PEER_KEYS), :],
                                 preferred_element_type=F32))
        out0 = pl.multiple_of(hh * PEER_TOPK, PEER_TOPK)
        tiles = [slice(t * LANES, (t + 1) * LANES) for t in range(n_tiles)]
        tops = _topk_rows([sides[s][:, lanes] for lanes in tiles for s in range(2)], PEER_TOPK)
        problems = [_pair_candidates(*tops[2 * t], *tops[2 * t + 1]) for t in range(n_tiles)]
        for lanes, (best, ids) in zip(tiles, _topk_pairs(problems, PEER_TOPK)):
            ex = jnp.exp(best - jnp.max(best, axis=0, keepdims=True))
            gate = ex / jnp.sum(ex, axis=0, keepdims=True)
            e_ref[pl.ds(out0, PEER_TOPK), lanes] = ids.astype(jnp.int32)
            g_ref[pl.ds(out0, PEER_TOPK), lanes] = gate
        return 0

    lax.fori_loop(0, PEER_HEADS, head, 0)


def _peer_topk_call(h2, wq_t, keys):
    tq = TQ_TOPK
    n_slots = PEER_HEADS * PEER_TOPK
    return pl.pallas_call(
        _peer_topk_kernel, grid=(SEQ // tq,),
        in_specs=[_tile_spec(tq, D_MODEL), _full_spec((D_MODEL, D_MODEL)),
                  _full_spec((2 * PEER_HEADS, PEER_KEYS, PEER_KEYS))],
        out_specs=[pl.BlockSpec((n_slots, tq), lambda i: (0, i)),
                   pl.BlockSpec((n_slots, tq), lambda i: (0, i))],
        out_shape=[jax.ShapeDtypeStruct((n_slots, SEQ), jnp.int32),
                   jax.ShapeDtypeStruct((n_slots, SEQ), F32)],
        scratch_shapes=[pltpu.VMEM((D_MODEL, tq), BF16)],
        compiler_params=_cparams(("arbitrary",)), name="peer_topk",
    )(h2, wq_t, keys)


def _peer_gate_kernel(e_ref, g_ref, o_ref):
    n_slots = e_ref.shape[1]
    kio = lax.broadcasted_iota(jnp.int32, (PEER_KEYS, n_slots), 0)

    def token(t, _):
        e = e_ref[pl.ds(t, 1), :]
        gate = g_ref[pl.ds(t, 1), :]
        i1 = e >> 7
        i2 = e & (PEER_KEYS - 1)
        oh1 = jnp.where(i1 == kio, 1.0, 0.0).astype(BF16)
        w2 = jnp.where(i2 == kio, gate, 0.0).astype(BF16)
        o_ref[t] = lax.dot_general(oh1, w2, (((1,), (1,)), ((), ())),
                                   preferred_element_type=F32)
        return 0

    lax.fori_loop(0, e_ref.shape[0], token, 0, unroll=GATE_UNROLL)


def _peer_gate_call(e_rows, g_rows):
    tg = TG_BUILD
    n_slots = PEER_HEADS * PEER_TOPK
    return pl.pallas_call(
        _peer_gate_kernel, grid=(SEQ // tg,),
        in_specs=[_tile_spec(tg, n_slots), _tile_spec(tg, n_slots)],
        out_specs=pl.BlockSpec((tg, PEER_KEYS, PEER_KEYS), lambda i: (i, 0, 0)),
        out_shape=jax.ShapeDtypeStruct((SEQ, PEER_KEYS, PEER_KEYS), F32),
        compiler_params=_cparams(("arbitrary",)), name="peer_gate",
    )(e_rows, g_rows)


def _peer_dense_kernel(h2_ref, u_ref, v_ref, g_ref, o_ref):
    @pl.when(pl.program_id(1) == 0)
    def _():
        o_ref[...] = jnp.zeros_like(o_ref)

    a = lax.dot_general(h2_ref[...], u_ref[...], (((1,), (1,)), ((), ())),
                        preferred_element_type=F32)
    k0 = (pl.program_id(1) % G_BLOCKS_PER_FETCH) * KEY1_PER_BLOCK
    gate = jnp.concatenate([g_ref[:, k0 + k, :] for k in range(KEY1_PER_BLOCK)], axis=1)
    p = (gate * jax.nn.gelu(a)).astype(BF16)
    o_ref[...] += jnp.dot(p, v_ref[...], preferred_element_type=F32)


def _peer_dense_call(h2, u_tab, v_tab, gmat):
    tb, eb = TB_DENSE, EB_DENSE
    return pl.pallas_call(
        _peer_dense_kernel, grid=(SEQ // tb, PEER_EXPERTS // eb),
        in_specs=[pl.BlockSpec((tb, D_MODEL), lambda i, j: (i, 0)),
                  pl.BlockSpec((eb, D_MODEL), lambda i, j: (j, 0)),
                  pl.BlockSpec((eb, D_MODEL), lambda i, j: (j, 0)),
                  pl.BlockSpec((tb, KEY1_PER_BLOCK * G_BLOCKS_PER_FETCH, PEER_KEYS),
                               lambda i, j: (i, j // G_BLOCKS_PER_FETCH, 0))],
        out_specs=pl.BlockSpec((tb, D_MODEL), lambda i, j: (i, 0)),
        out_shape=jax.ShapeDtypeStruct((SEQ, D_MODEL), F32),
        compiler_params=_cparams(("arbitrary", "arbitrary")), name="peer_dense",
    )(h2, u_tab, v_tab, gmat)


def _final_kernel(x_ref, p_ref, g2_ref, fg_ref, o_ref):
    x = x_ref[...] + g2_ref[...] * p_ref[...]
    o_ref[...] = _rms(x, fg_ref[...])


def _final_call(x, peer, g2, fg):
    tm = TM_FINAL
    return pl.pallas_call(
        _final_kernel, grid=(SEQ // tm,),
        in_specs=[_tile_spec(tm, D_MODEL), _tile_spec(tm, D_MODEL), _row_spec(D_MODEL),
                  _row_spec(D_MODEL)],
        out_specs=_tile_spec(tm, D_MODEL),
        out_shape=jax.ShapeDtypeStruct((SEQ, D_MODEL), F32),
        compiler_params=_cparams(("arbitrary",)), name="final_norm",
    )(x, peer, g2, fg)


def _chunk_causal_mask(n):
    blk = jnp.arange(n) // CHUNK
    return blk[:, None] >= blk[None, :]


def kernel(x, c, w_mod, b_mod, norm1_g, w_in, b_in, gm_ln_g, gm_ln_b, gm_ws, gm_bs, ssm_lam_re, ssm_lam_im, ssm_log_dt, ssm_b_re, ssm_b_im, ssm_c_re, ssm_c_im, ssm_d, glu_w, glu_b, branch_g, w_out, norm2_g, peer_wq, peer_keys, peer_u, peer_v, final_g):
    assert x.shape == (1, SEQ, D_MODEL)
    xt = x.reshape(SEQ, D_MODEL)
    mod = _mod_call(c, w_mod, b_mod).reshape(DEPTH, N_MOD, 1, D_MODEL)
    row = lambda v: v.reshape(1, -1)
    res = None
    for i in range(DEPTH):
        sh1, sc1, g1, sh2, sc2, g2 = (mod[i, k] for k in range(N_MOD))
        wm = jnp.where(_chunk_causal_mask(GMLP_BLOCK)[None], gm_ws[i], 0.0).astype(BF16)
        bs_full = jnp.repeat(gm_bs[i].T, LANES, axis=1)
        xt, ma, xb = _mix_in_call(
            xt, res, row(norm1_g[i]), sc1, sh1, w_in[i].astype(BF16), row(b_in[i]),
            row(gm_ln_g[i]), row(gm_ln_b[i]), wm, bs_full, row(branch_g[i, :WIDTH_A]))
        prep = _s5_prep(ssm_lam_re[i], ssm_lam_im[i], ssm_log_dt[i], ssm_b_re[i], ssm_b_im[i],
                        ssm_c_re[i], ssm_c_im[i])
        ys = _s5_call(xb, prep)
        xt, h2 = _mix_out_call(
            xt, ma, xb, ys, row(ssm_d[i]), glu_w[i].astype(BF16), row(glu_b[i]),
            row(branch_g[i, WIDTH_A:]), w_out[i].astype(BF16), g1, row(norm2_g[i]), sc2, sh2)
        keys = peer_keys[i].reshape(2 * PEER_HEADS, PEER_KEYS, PEER_KEYS).astype(BF16)
        e_t, gate_t = _peer_topk_call(h2, peer_wq[i].T.astype(BF16), keys)
        gmat = _peer_gate_call(e_t.T, gate_t.T)
        peer = _peer_dense_call(h2, peer_u[i].astype(BF16), peer_v[i].astype(BF16), gmat)
        res = (peer, g2)
    out = _final_call(xt, res[0], res[1], row(final_g))
    return out.reshape(1, SEQ, D_MODEL)
```

```python
import functools
import math

import jax
import jax.numpy as jnp
from jax import lax
from jax.experimental import pallas as pl
from jax.experimental.pallas import tpu as pltpu

F32 = jnp.float32
BF16 = jnp.bfloat16

D_MODEL = 2048
SEQ = 16384
DEPTH = 2
CHUNK = 64
WIDTH_A = 1024
WIDTH_B = 1024
IN_WIDTH = 3072
GMLP_BLOCK = 128
GMLP_HEADS = 8
SSM_GROUP = 16
SSM_GROUPS = 64
SSM_STATE = 64
PEER_HEADS = 8
PEER_KEYS = 128
PEER_EXPERTS = PEER_KEYS * PEER_KEYS
PEER_TOPK = 16
N_MOD = 6
EPS = 1e-6

LANES = 128
N_CHUNKS = SEQ // CHUNK
CHUNK_W = CHUNK * SSM_GROUP
VMEM_LIMIT = 56 * 1024 * 1024

TM_MIX = 256
TQ_TOPK = 256
TG_BUILD = 64
GATE_UNROLL = 8
TB_DENSE = 1024
EB_DENSE = 1024
EC_DENSE = 512
KEY1_PER_BLOCK = EB_DENSE // PEER_KEYS
F32_SUBLANES = 8
GATE_PITCH = PEER_KEYS + F32_SUBLANES
TM_FINAL = 512
TN_MOD = 1024

_NEG_INF = float("-inf")
_BIG = float(1 << 20)


def _cparams(sem):
    return pltpu.CompilerParams(dimension_semantics=sem, vmem_limit_bytes=VMEM_LIMIT)


def _rms(x, g):
    return x * lax.rsqrt(jnp.mean(x * x, axis=-1, keepdims=True) + EPS) * g


def _mod_kernel(c_ref, w_ref, b_ref, o_ref):
    c = c_ref[...]
    cond = c / (1.0 + jnp.exp(-c))
    w = w_ref[0]
    o_ref[0] = jnp.sum(w * cond, axis=0, keepdims=True) + b_ref[0]


def _mod_call(c, w_mod, b_mod):
    n = N_MOD * D_MODEL
    return pl.pallas_call(
        _mod_kernel,
        grid=(DEPTH, n // TN_MOD),
        in_specs=[
            pl.BlockSpec((D_MODEL, 1), lambda l, j: (0, 0)),
            pl.BlockSpec((1, D_MODEL, TN_MOD), lambda l, j: (l, 0, j)),
            pl.BlockSpec((1, 1, TN_MOD), lambda l, j: (l, 0, j)),
        ],
        out_specs=pl.BlockSpec((1, 1, TN_MOD), lambda l, j: (l, 0, j)),
        out_shape=jax.ShapeDtypeStruct((DEPTH, 1, n), F32),
        compiler_params=_cparams(("arbitrary", "arbitrary")),
        name="mod_gemv",
    )(c.reshape(D_MODEL, 1), w_mod, b_mod.reshape(DEPTH, 1, n))


def _mix_in_body(x, n1g, sc1, sh1, win_ref, bin_ref, lng, lnb, wm_ref, bs_ref, bga,
                 ma_ref, xb_ref):
    h = (_rms(x, n1g) * (1.0 + sc1) + sh1).astype(BF16)
    proj = jnp.dot(h, win_ref[...], preferred_element_type=F32) + bin_ref[...]
    z = jax.nn.gelu(proj[:, :2 * WIDTH_A])
    u = z[:, :WIDTH_A]
    v = z[:, WIDTH_A:]
    mu = jnp.mean(v, axis=-1, keepdims=True)
    vc = v - mu
    var = jnp.mean(vc * vc, axis=-1, keepdims=True)
    vn = (vc * lax.rsqrt(var + EPS) * lng + lnb).astype(BF16)
    tm = x.shape[0]
    rows = []
    for n in range(tm // GMLP_BLOCK):
        vb = vn[n * GMLP_BLOCK:(n + 1) * GMLP_BLOCK, :]
        cols = []
        for hh in range(GMLP_HEADS):
            cols.append(jnp.dot(wm_ref[hh], vb[:, hh * LANES:(hh + 1) * LANES],
                                preferred_element_type=F32))
        rows.append(jnp.concatenate(cols, axis=1) + bs_ref[...])
    s = jnp.concatenate(rows, axis=0)
    out_a = u * s
    ma_ref[...] = _rms(out_a, bga).astype(BF16)
    xb_ref[...] = proj[:, 2 * WIDTH_A:]


def _mix_in_kernel0(x_ref, n1g, sc1, sh1, win_ref, bin_ref, lng, lnb, wm_ref, bs_ref, bga,
                    ma_ref, xb_ref):
    _mix_in_body(x_ref[...], n1g[...], sc1[...], sh1[...], win_ref, bin_ref, lng[...], lnb[...],
                 wm_ref, bs_ref, bga[...], ma_ref, xb_ref)


def _mix_in_kernel1(x_ref, p_ref, g2_ref, n1g, sc1, sh1, win_ref, bin_ref, lng, lnb, wm_ref,
                    bs_ref, bga, xo_ref, ma_ref, xb_ref):
    x = x_ref[...] + g2_ref[...] * p_ref[...]
    xo_ref[...] = x
    _mix_in_body(x, n1g[...], sc1[...], sh1[...], win_ref, bin_ref, lng[...], lnb[...],
                 wm_ref, bs_ref, bga[...], ma_ref, xb_ref)


def _row_spec(width):
    return pl.BlockSpec((1, width), lambda i: (0, 0))


def _tile_spec(tm, width):
    return pl.BlockSpec((tm, width), lambda i: (i, 0))


def _full_spec(shape):
    nd = len(shape)
    return pl.BlockSpec(shape, lambda i: (0,) * nd)


def _layer_spec(shape, layer):
    nd = len(shape)
    return pl.BlockSpec((None,) + tuple(shape), lambda i: (layer,) + (0,) * nd)


def _mix_in_call(x, res, n1g, sc1, sh1, w_in, layer, b_in, lng, lnb, wm, bs_full, bga):
    tm = TM_MIX
    common_specs = [
        _row_spec(D_MODEL), _row_spec(D_MODEL), _row_spec(D_MODEL),
        _layer_spec((D_MODEL, IN_WIDTH), layer), _row_spec(IN_WIDTH),
        _row_spec(WIDTH_A), _row_spec(WIDTH_A),
        _full_spec((GMLP_HEADS, GMLP_BLOCK, GMLP_BLOCK)),
        _full_spec((GMLP_BLOCK, WIDTH_A)),
        _row_spec(WIDTH_A),
    ]
    common_args = (n1g, sc1, sh1, w_in, b_in, lng, lnb, wm, bs_full, bga)
    out_specs = [_tile_spec(tm, WIDTH_A), _tile_spec(tm, WIDTH_B)]
    out_shape = [jax.ShapeDtypeStruct((SEQ, WIDTH_A), BF16),
                 jax.ShapeDtypeStruct((SEQ, WIDTH_B), F32)]
    if res is None:
        ma, xb = pl.pallas_call(
            _mix_in_kernel0, grid=(SEQ // tm,),
            in_specs=[_tile_spec(tm, D_MODEL)] + common_specs,
            out_specs=out_specs, out_shape=out_shape,
            compiler_params=_cparams(("arbitrary",)), name="mix_in",
        )(x, *common_args)
        return x, ma, xb
    peer, g2 = res
    xo, ma, xb = pl.pallas_call(
        _mix_in_kernel1, grid=(SEQ // tm,),
        in_specs=[_tile_spec(tm, D_MODEL), _tile_spec(tm, D_MODEL), _row_spec(D_MODEL)]
        + common_specs,
        out_specs=[_tile_spec(tm, D_MODEL)] + out_specs,
        out_shape=[jax.ShapeDtypeStruct((SEQ, D_MODEL), F32)] + out_shape,
        compiler_params=_cparams(("arbitrary",)), name="mix_in_res",
    )(x, peer, g2, *common_args)
    return xo, ma, xb


def _s5_local_kernel(x_ref, w_ref, o_ref):
    o_ref[...] = jnp.dot(x_ref[0], w_ref[0], preferred_element_type=F32)


def _s5_scan_kernel(l_ref, ca_ref, cb_ref, h_ref):
    ca = ca_ref[...]
    cb = cb_ref[...]
    n_half = ca.shape[1] // LANES

    def step(k, s):
        h_ref[k] = s
        pieces = []
        for j in range(0, n_half, 2):
            pieces.append(s[:, (j + 1) * LANES:(j + 2) * LANES])
            pieces.append(s[:, j * LANES:(j + 1) * LANES])
        sw = jnp.concatenate(pieces, axis=1)
        return ca * s + cb * sw + l_ref[k]

    lax.fori_loop(0, N_CHUNKS, step, jnp.zeros(ca.shape, F32))


def _s5_out_kernel(x_ref, m_ref, h_ref, wo_ref, y_ref):
    y = jnp.dot(x_ref[0], m_ref[0], preferred_element_type=F32)
    h = h_ref[...]
    h_hi = h.astype(BF16)
    h_lo = (h - h_hi.astype(F32)).astype(BF16)
    y += jnp.dot(h_hi, wo_ref[0], preferred_element_type=F32)
    y += jnp.dot(h_lo, wo_ref[0], preferred_element_type=F32)
    y_ref[0] = y


def _s5_call(xb, prep):
    m_toep, w_state, w_out, ca, cb = prep
    g = SSM_GROUPS
    xg = xb.reshape(N_CHUNKS, CHUNK, g, SSM_GROUP).transpose(2, 0, 1, 3)
    xg = xg.reshape(g, N_CHUNKS, CHUNK_W).astype(BF16)
    local = pl.pallas_call(
        _s5_local_kernel, grid=(g,),
        in_specs=[pl.BlockSpec((1, N_CHUNKS, CHUNK_W), lambda i: (i, 0, 0)),
                  pl.BlockSpec((1, CHUNK_W, 4 * SSM_STATE), lambda i: (i, 0, 0))],
        out_specs=pl.BlockSpec((N_CHUNKS, 4 * SSM_STATE), lambda i: (0, i)),
        out_shape=jax.ShapeDtypeStruct((N_CHUNKS, g * 4 * SSM_STATE), F32),
        compiler_params=_cparams(("arbitrary",)), name="s5_local",
    )(xg, w_state)
    rows = 8
    width = g * 4 * SSM_STATE // rows
    hstart = pl.pallas_call(
        _s5_scan_kernel,
        out_shape=jax.ShapeDtypeStruct((N_CHUNKS, rows, width), F32),
        compiler_params=pltpu.CompilerParams(vmem_limit_bytes=VMEM_LIMIT), name="s5_scan",
    )(local.reshape(N_CHUNKS, rows, width), ca, cb)
    hstart = hstart.reshape(N_CHUNKS, g * 4 * SSM_STATE)
    yg = pl.pallas_call(
        _s5_out_kernel, grid=(g,),
        in_specs=[pl.BlockSpec((1, N_CHUNKS, CHUNK_W), lambda i: (i, 0, 0)),
                  pl.BlockSpec((1, CHUNK_W, CHUNK_W), lambda i: (i, 0, 0)),
                  pl.BlockSpec((N_CHUNKS, 2 * SSM_STATE), lambda i: (0, 2 * i)),
                  pl.BlockSpec((1, 2 * SSM_STATE, CHUNK_W), lambda i: (i, 0, 0))],
        out_specs=pl.BlockSpec((1, N_CHUNKS, CHUNK_W), lambda i: (i, 0, 0)),
        out_shape=jax.ShapeDtypeStruct((g, N_CHUNKS, CHUNK_W), F32),
        compiler_params=_cparams(("arbitrary",)), name="s5_out",
    )(xg, m_toep, hstart, w_out)
    y = yg.reshape(g, N_CHUNKS, CHUNK, SSM_GROUP).transpose(1, 2, 0, 3)
    return y.reshape(SEQ, WIDTH_B)


def _s5_prep(lam_re, lam_im, log_dt, b_re, b_im, c_re, c_im):
    hp = lax.Precision.HIGHEST
    cmul = lambda ar, ai, br, bi: (ar * br - ai * bi, ar * bi + ai * br)
    dt = jnp.exp(log_dt)[:, None]
    steps = jnp.arange(0, CHUNK + 1, dtype=F32)[:, None, None]
    mag = jnp.exp(lam_re[None] * (dt[None] * steps))
    ang = lam_im[None] * (dt[None] * steps)
    pw_r, pw_i = mag * jnp.cos(ang), mag * jnp.sin(ang)
    n_r, n_i = pw_r[1] - 1.0, pw_i[1]
    den = lam_re * lam_re + lam_im * lam_im
    q_r = (n_r * lam_re + n_i * lam_im) / den
    q_i = (n_i * lam_re - n_r * lam_im) / den
    bb_r, bb_i = cmul(q_r[..., None], q_i[..., None], b_re, b_im)
    e_r, e_i = cmul(pw_r[:CHUNK, :, :, None], pw_i[:CHUNK, :, :, None], bb_r[None], bb_i[None])
    k_tap = (jnp.einsum("ghp,tgpk->tghk", c_re, e_r, precision=hp)
             - jnp.einsum("ghp,tgpk->tghk", c_im, e_i, precision=hp))
    krow = k_tap.transpose(1, 3, 0, 2).reshape(SSM_GROUPS, SSM_GROUP, CHUNK_W).astype(BF16)
    kpad = jnp.concatenate([jnp.zeros_like(krow), krow], axis=-1)
    m_toep = jnp.stack([kpad[:, :, CHUNK_W - SSM_GROUP * c:2 * CHUNK_W - SSM_GROUP * c]
                        for c in range(CHUNK)], axis=1).reshape(SSM_GROUPS, CHUNK_W, CHUNK_W)
    rev_r, rev_i = pw_r[CHUNK - 1::-1][:CHUNK], pw_i[CHUNK - 1::-1][:CHUNK]
    w_r, w_i = cmul(rev_r[..., None], rev_i[..., None], bb_r[None], bb_i[None])
    lay = lambda w: w.transpose(1, 0, 3, 2).reshape(SSM_GROUPS, CHUNK_W, SSM_STATE)
    w_r, w_i = lay(w_r), lay(w_i)
    w_state = jnp.concatenate([w_r, w_i, w_i, w_r], axis=-1).astype(BF16)
    ca_r, ca_i = cmul(c_re[None], c_im[None], pw_r[1:CHUNK + 1, :, None, :],
                      pw_i[1:CHUNK + 1, :, None, :])
    lay = lambda w: w.transpose(1, 3, 0, 2).reshape(SSM_GROUPS, SSM_STATE, CHUNK_W)
    w_out = jnp.concatenate([lay(ca_r), -lay(ca_i)], axis=1).astype(BF16)
    ar, ai = pw_r[CHUNK], pw_i[CHUNK]
    ca = jnp.concatenate([ar, ar, ar, ar], axis=-1).reshape(8, -1)
    cb = jnp.concatenate([-ai, ai, ai, -ai], axis=-1).reshape(8, -1)
    return m_toep, w_state, w_out, ca, cb


def _mix_out_kernel(x_ref, ma_ref, xb_ref, ys_ref, d_ref, gw_ref, gb_ref, bgb_ref, wo_ref,
                    g1_ref, n2g_ref, sc2_ref, sh2_ref, xo_ref, h2_ref):
    y = ys_ref[...] + d_ref[...] * xb_ref[...]
    g = jax.nn.gelu(y)
    gate = jax.nn.sigmoid(jnp.dot(g.astype(BF16), gw_ref[...], preferred_element_type=F32)
                          + gb_ref[...])
    mb = _rms(g * gate, bgb_ref[...]).astype(BF16)
    mo = jnp.dot(ma_ref[...], wo_ref[:WIDTH_A, :], preferred_element_type=F32)
    mo += jnp.dot(mb, wo_ref[WIDTH_A:, :], preferred_element_type=F32)
    x = x_ref[...] + g1_ref[...] * mo
    xo_ref[...] = x
    h2_ref[...] = (_rms(x, n2g_ref[...]) * (1.0 + sc2_ref[...]) + sh2_ref[...]).astype(BF16)


def _mix_out_call(x, ma, xb, ys, d, glu_w, layer, glu_b, bgb, w_out, g1, n2g, sc2, sh2):
    tm = TM_MIX
    return pl.pallas_call(
        _mix_out_kernel, grid=(SEQ // tm,),
        in_specs=[_tile_spec(tm, D_MODEL), _tile_spec(tm, WIDTH_A), _tile_spec(tm, WIDTH_B),
                  _tile_spec(tm, WIDTH_B), _row_spec(WIDTH_B),
                  _layer_spec((WIDTH_B, WIDTH_B), layer), _row_spec(WIDTH_B), _row_spec(WIDTH_B),
                  _layer_spec((D_MODEL, D_MODEL), layer),
                  _row_spec(D_MODEL), _row_spec(D_MODEL), _row_spec(D_MODEL), _row_spec(D_MODEL)],
        out_specs=[_tile_spec(tm, D_MODEL), _tile_spec(tm, D_MODEL)],
        out_shape=[jax.ShapeDtypeStruct((SEQ, D_MODEL), F32),
                   jax.ShapeDtypeStruct((SEQ, D_MODEL), BF16)],
        compiler_params=_cparams(("arbitrary",)), name="mix_out",
    )(x, ma, xb, ys, d, glu_w, glu_b, bgb, w_out, g1, n2g, sc2, sh2)


def _topk_rows(xs, k):
    n, w = xs[0].shape
    riota = lax.broadcasted_iota(jnp.int32, (n, w), 0).astype(F32)
    kiota = lax.broadcasted_iota(jnp.int32, (k, w), 0)

    def body(it, carry):
        out = []
        for x, vals, idxs in carry:
            m = jnp.max(x, axis=0, keepdims=True)
            am = jnp.min(jnp.where(x == m, riota, float(n)), axis=0, keepdims=True)
            x = jnp.where(riota == am, _NEG_INF, x)
            vals = jnp.where(kiota == it, m, vals)
            idxs = jnp.where(kiota == it, am, idxs)
            out.append((x, vals, idxs))
        return tuple(out)

    zeros = jnp.zeros((k, w), F32)
    res = lax.fori_loop(0, k, body, tuple((x, zeros, zeros) for x in xs))
    return [(vals, idxs) for _, vals, idxs in res]


def _pair_candidates(v1, i1, v2, i2):
    k, w = v1.shape
    r = lax.broadcasted_iota(jnp.int32, (k, w), 0)
    rf = r.astype(F32)
    cands, flats, ids = [], [], []

    def add(valid, s, flat, e):
        cands.append(jnp.where(valid, s, _NEG_INF))
        flats.append(jnp.where(valid, flat, _BIG))
        ids.append(e)

    for a in range(4):
        add(r < PEER_TOPK // (a + 1), v1[a:a + 1, :] + v2, float(a * PEER_TOPK) + rf,
            i1[a:a + 1, :] * float(PEER_KEYS) + i2)
    for b in range(3):
        hi = PEER_TOPK // (b + 1)
        valid = jnp.logical_and(r >= 4, r < hi)
        add(valid, v1 + v2[b:b + 1, :], rf * float(PEER_TOPK) + float(b),
            i1 * float(PEER_KEYS) + i2[b:b + 1, :])
    return (jnp.concatenate(cands, axis=0), jnp.concatenate(flats, axis=0),
            jnp.concatenate(ids, axis=0))


def _topk_pairs(problems, k):
    w = problems[0][0].shape[1]
    kiota = lax.broadcasted_iota(jnp.int32, (k, w), 0)
    statics = [(flat, ids) for _, flat, ids in problems]

    def body(it, carry):
        out = []
        for (cand, vals, sel_ids), (flat, ids) in zip(carry, statics):
            m = jnp.max(cand, axis=0, keepdims=True)
            fm = jnp.min(jnp.where(cand == m, flat, _BIG), axis=0, keepdims=True)
            sel = flat == fm
            e = jnp.max(jnp.where(sel, ids, -1.0), axis=0, keepdims=True)
            cand = jnp.where(sel, _NEG_INF, cand)
            vals = jnp.where(kiota == it, m, vals)
            sel_ids = jnp.where(kiota == it, e, sel_ids)
            out.append((cand, vals, sel_ids))
        return tuple(out)

    zeros = jnp.zeros((k, w), F32)
    res = lax.fori_loop(0, k, body, tuple((cand, zeros, zeros) for cand, _, _ in problems))
    return [(vals, sel_ids) for _, vals, sel_ids in res]


def _peer_topk_kernel(h2_ref, wqt_ref, keys_ref, e_ref, g_ref, qt_ref):
    qt_ref[...] = lax.dot_general(wqt_ref[...], h2_ref[...], (((1,), (1,)), ((), ())),
                                  preferred_element_type=F32).astype(BF16)
    n_tiles = h2_ref.shape[0] // LANES

    def head(hh, _):
        sides = []
        for s in range(2):
            row0 = pl.multiple_of(hh * (2 * PEER_KEYS) + s * PEER_KEYS, PEER_KEYS)
            sides.append(jnp.dot(keys_ref[hh * 2 + s], qt_ref[pl.ds(row0, PEER_KEYS), :],
                                 preferred_element_type=F32))
        out0 = pl.multiple_of(hh * PEER_TOPK, PEER_TOPK)
        tiles = [slice(t * LANES, (t + 1) * LANES) for t in range(n_tiles)]
        tops = _topk_rows([sides[s][:, lanes] for lanes in tiles for s in range(2)], PEER_TOPK)
        problems = [_pair_candidates(*tops[2 * t], *tops[2 * t + 1]) for t in range(n_tiles)]
        for lanes, (best, ids) in zip(tiles, _topk_pairs(problems, PEER_TOPK)):
            ex = jnp.exp(best - jnp.max(best, axis=0, keepdims=True))
            gate = ex / jnp.sum(ex, axis=0, keepdims=True)
            e_ref[pl.ds(out0, PEER_TOPK), lanes] = ids.astype(jnp.int32)
            g_ref[pl.ds(out0, PEER_TOPK), lanes] = gate
        return 0

    lax.fori_loop(0, PEER_HEADS, head, 0)


def _peer_topk_call(h2, wq_t, layer, keys):
    tq = TQ_TOPK
    n_slots = PEER_HEADS * PEER_TOPK
    return pl.pallas_call(
        _peer_topk_kernel, grid=(SEQ // tq,),
        in_specs=[_tile_spec(tq, D_MODEL), _layer_spec((D_MODEL, D_MODEL), layer),
                  _full_spec((2 * PEER_HEADS, PEER_KEYS, PEER_KEYS))],
        out_specs=[pl.BlockSpec((n_slots, tq), lambda i: (0, i)),
                   pl.BlockSpec((n_slots, tq), lambda i: (0, i))],
        out_shape=[jax.ShapeDtypeStruct((n_slots, SEQ), jnp.int32),
                   jax.ShapeDtypeStruct((n_slots, SEQ), F32)],
        scratch_shapes=[pltpu.VMEM((D_MODEL, tq), BF16)],
        compiler_params=_cparams(("arbitrary",)), name="peer_topk",
    )(h2, wq_t, keys)


def _peer_gate_kernel(e_ref, g_ref, o_ref, tok_ref):
    tg, n_slots = e_ref.shape
    kio = lax.broadcasted_iota(jnp.int32, (PEER_KEYS, n_slots), 0)

    def token(t, _):
        e = e_ref[pl.ds(t, 1), :]
        gate = g_ref[pl.ds(t, 1), :]
        i1 = e >> 7
        i2 = e & (PEER_KEYS - 1)
        oh1 = jnp.where(i1 == kio, 1.0, 0.0).astype(BF16)
        w2 = jnp.where(i2 == kio, gate, 0.0).astype(BF16)
        row0 = pl.multiple_of(t * GATE_PITCH, F32_SUBLANES)
        tok_ref[pl.ds(row0, PEER_KEYS), :] = lax.dot_general(
            oh1, w2, (((1,), (1,)), ((), ())), preferred_element_type=F32)
        return 0

    lax.fori_loop(0, tg, token, 0, unroll=GATE_UNROLL)

    def plane(k, _):
        o_ref[k] = tok_ref[pl.ds(k, tg, stride=GATE_PITCH), :]
        return 0

    lax.fori_loop(0, PEER_KEYS, plane, 0, unroll=GATE_UNROLL)


def _peer_gate_call(e_rows, g_rows):
    tg = TG_BUILD
    n_slots = PEER_HEADS * PEER_TOPK
    return pl.pallas_call(
        _peer_gate_kernel, grid=(SEQ // tg,),
        in_specs=[_tile_spec(tg, n_slots), _tile_spec(tg, n_slots)],
        out_specs=pl.BlockSpec((PEER_KEYS, tg, PEER_KEYS), lambda i: (0, i, 0)),
        out_shape=jax.ShapeDtypeStruct((PEER_KEYS, SEQ, PEER_KEYS), F32),
        scratch_shapes=[pltpu.VMEM((tg * GATE_PITCH, PEER_KEYS), F32)],
        compiler_params=_cparams(("arbitrary",)), name="peer_gate",
    )(e_rows, g_rows)


def _peer_dense_kernel(h2_ref, ut_ref, v_ref, g_ref, o_ref):
    @pl.when(pl.program_id(1) == 0)
    def _():
        o_ref[...] = jnp.zeros_like(o_ref)

    h2 = h2_ref[...]
    keys_per_chunk = EC_DENSE // PEER_KEYS
    n_chunks = EB_DENSE // EC_DENSE
    chunk = lambda c: slice(c * EC_DENSE, (c + 1) * EC_DENSE)
    score = lambda c: jnp.dot(h2, ut_ref[:, chunk(c)], preferred_element_type=F32)
    a_next = score(0)
    for c in range(n_chunks):
        a = a_next
        if c + 1 < n_chunks:
            a_next = score(c + 1)
        gate = jnp.concatenate([g_ref[c * keys_per_chunk + k] for k in range(keys_per_chunk)],
                               axis=1)
        p = (gate * jax.nn.gelu(a)).astype(BF16)
        o_ref[...] += jnp.dot(p, v_ref[chunk(c), :], preferred_element_type=F32)


def _peer_dense_call(h2, ut_tab, v_tab, layer, gmat):
    tb, eb = TB_DENSE, EB_DENSE
    return pl.pallas_call(
        _peer_dense_kernel, grid=(SEQ // tb, PEER_EXPERTS // eb),
        in_specs=[pl.BlockSpec((tb, D_MODEL), lambda i, j: (i, 0)),
                  pl.BlockSpec((None, D_MODEL, eb), lambda i, j: (layer, 0, j)),
                  pl.BlockSpec((None, eb, D_MODEL), lambda i, j: (layer, j, 0)),
                  pl.BlockSpec((KEY1_PER_BLOCK, tb, PEER_KEYS), lambda i, j: (j, i, 0))],
        out_specs=pl.BlockSpec((tb, D_MODEL), lambda i, j: (i, 0)),
        out_shape=jax.ShapeDtypeStruct((SEQ, D_MODEL), F32),
        compiler_params=_cparams(("arbitrary", "arbitrary")), name="peer_dense",
    )(h2, ut_tab, v_tab, gmat)


def _final_kernel(x_ref, p_ref, g2_ref, fg_ref, o_ref):
    x = x_ref[...] + g2_ref[...] * p_ref[...]
    o_ref[...] = _rms(x, fg_ref[...])


def _final_call(x, peer, g2, fg):
    tm = TM_FINAL
    return pl.pallas_call(
        _final_kernel, grid=(SEQ // tm,),
        in_specs=[_tile_spec(tm, D_MODEL), _tile_spec(tm, D_MODEL), _row_spec(D_MODEL),
                  _row_spec(D_MODEL)],
        out_specs=_tile_spec(tm, D_MODEL),
        out_shape=jax.ShapeDtypeStruct((SEQ, D_MODEL), F32),
        compiler_params=_cparams(("arbitrary",)), name="final_norm",
    )(x, peer, g2, fg)


def _chunk_causal_mask(n):
    blk = jnp.arange(n) // CHUNK
    return blk[:, None] >= blk[None, :]


def kernel(x, c, w_mod, b_mod, norm1_g, w_in, b_in, gm_ln_g, gm_ln_b, gm_ws, gm_bs, ssm_lam_re, ssm_lam_im, ssm_log_dt, ssm_b_re, ssm_b_im, ssm_c_re, ssm_c_im, ssm_d, glu_w, glu_b, branch_g, w_out, norm2_g, peer_wq, peer_keys, peer_u, peer_v, final_g):
    assert x.shape == (1, SEQ, D_MODEL)
    xt = x.reshape(SEQ, D_MODEL)
    mod = _mod_call(c, w_mod, b_mod).reshape(DEPTH, N_MOD, 1, D_MODEL)
    row = lambda v: v.reshape(1, -1)
    res = None
    w_in_b, glu_w_b, w_out_b = w_in.astype(BF16), glu_w.astype(BF16), w_out.astype(BF16)
    wq_t_b = peer_wq.astype(BF16).transpose(0, 2, 1)
    ut_b = peer_u.astype(BF16).transpose(0, 2, 1)
    v_b = peer_v.astype(BF16)
    for i in range(DEPTH):
        sh1, sc1, g1, sh2, sc2, g2 = (mod[i, k] for k in range(N_MOD))
        wm = jnp.where(_chunk_causal_mask(GMLP_BLOCK)[None], gm_ws[i], 0.0).astype(BF16)
        bs_full = jnp.repeat(gm_bs[i].T, LANES, axis=1)
        xt, ma, xb = _mix_in_call(
            xt, res, row(norm1_g[i]), sc1, sh1, w_in_b, i, row(b_in[i]),
            row(gm_ln_g[i]), row(gm_ln_b[i]), wm, bs_full, row(branch_g[i, :WIDTH_A]))
        prep = _s5_prep(ssm_lam_re[i], ssm_lam_im[i], ssm_log_dt[i], ssm_b_re[i], ssm_b_im[i],
                        ssm_c_re[i], ssm_c_im[i])
        ys = _s5_call(xb, prep)
        xt, h2 = _mix_out_call(
            xt, ma, xb, ys, row(ssm_d[i]), glu_w_b, i, row(glu_b[i]),
            row(branch_g[i, WIDTH_A:]), w_out_b, g1, row(norm2_g[i]), sc2, sh2)
        keys = peer_keys[i].reshape(2 * PEER_HEADS, PEER_KEYS, PEER_KEYS).astype(BF16)
        e_t, gate_t = _peer_topk_call(h2, wq_t_b, i, keys)
        gmat = _peer_gate_call(e_t.T, gate_t.T)
        peer = _peer_dense_call(h2, ut_b, v_b, i, gmat)
        res = (peer, g2)
    out = _final_call(xt, res[0], res[1], row(final_g))
    return out.reshape(1, SEQ, D_MODEL)
```

```python
import functools
import math

import jax
import jax.numpy as jnp
from jax import lax
from jax.experimental import pallas as pl
from jax.experimental.pallas import tpu as pltpu

F32 = jnp.float32
BF16 = jnp.bfloat16

D_MODEL = 2048
SEQ = 16384
DEPTH = 2
CHUNK = 64
WIDTH_A = 1024
WIDTH_B = 1024
IN_WIDTH = 3072
GMLP_BLOCK = 128
GMLP_HEADS = 8
SSM_GROUP = 16
SSM_GROUPS = 64
SSM_STATE = 64
PEER_HEADS = 8
PEER_KEYS = 128
PEER_EXPERTS = PEER_KEYS * PEER_KEYS
PEER_TOPK = 16
N_MOD = 6
EPS = 1e-6

LANES = 128
N_CHUNKS = SEQ // CHUNK
CHUNK_W = CHUNK * SSM_GROUP
VMEM_LIMIT = 56 * 1024 * 1024

TM_MIX = 256
TQ_TOPK = 256
TG_BUILD = 64
GATE_UNROLL = 64
TB_DENSE = 1024
EB_DENSE = 1024
EC_DENSE = 512
KEY1_PER_BLOCK = EB_DENSE // PEER_KEYS
F32_SUBLANES = 8
GATE_PITCH = PEER_KEYS + F32_SUBLANES
TM_FINAL = 512
TN_MOD = 1024

_NEG_INF = float("-inf")


def _cparams(sem):
    return pltpu.CompilerParams(dimension_semantics=sem, vmem_limit_bytes=VMEM_LIMIT)


def _rms(x, g):
    return x * lax.rsqrt(jnp.mean(x * x, axis=-1, keepdims=True) + EPS) * g


def _mod_kernel(c_ref, w_ref, b_ref, o_ref):
    c = c_ref[...]
    cond = c / (1.0 + jnp.exp(-c))
    w = w_ref[0]
    o_ref[0] = jnp.sum(w * cond, axis=0, keepdims=True) + b_ref[0]


def _mod_call(c, w_mod, b_mod):
    n = N_MOD * D_MODEL
    return pl.pallas_call(
        _mod_kernel,
        grid=(DEPTH, n // TN_MOD),
        in_specs=[
            pl.BlockSpec((D_MODEL, 1), lambda l, j: (0, 0)),
            pl.BlockSpec((1, D_MODEL, TN_MOD), lambda l, j: (l, 0, j)),
            pl.BlockSpec((1, 1, TN_MOD), lambda l, j: (l, 0, j)),
        ],
        out_specs=pl.BlockSpec((1, 1, TN_MOD), lambda l, j: (l, 0, j)),
        out_shape=jax.ShapeDtypeStruct((DEPTH, 1, n), F32),
        compiler_params=_cparams(("arbitrary", "arbitrary")),
        name="mod_gemv",
    )(c.reshape(D_MODEL, 1), w_mod, b_mod.reshape(DEPTH, 1, n))


def _mix_in_body(x, n1g, sc1, sh1, win_ref, bin_ref, lng, lnb, wm_ref, bs_ref, bga,
                 ma_ref, xb_ref):
    h = (_rms(x, n1g) * (1.0 + sc1) + sh1).astype(BF16)
    proj = jnp.dot(h, win_ref[...], preferred_element_type=F32) + bin_ref[...]
    z = jax.nn.gelu(proj[:, :2 * WIDTH_A])
    u = z[:, :WIDTH_A]
    v = z[:, WIDTH_A:]
    mu = jnp.mean(v, axis=-1, keepdims=True)
    vc = v - mu
    var = jnp.mean(vc * vc, axis=-1, keepdims=True)
    vn = (vc * lax.rsqrt(var + EPS) * lng + lnb).astype(BF16)
    tm = x.shape[0]
    rows = []
    for n in range(tm // GMLP_BLOCK):
        vb = vn[n * GMLP_BLOCK:(n + 1) * GMLP_BLOCK, :]
        cols = []
        for hh in range(GMLP_HEADS):
            cols.append(jnp.dot(wm_ref[hh], vb[:, hh * LANES:(hh + 1) * LANES],
                                preferred_element_type=F32))
        rows.append(jnp.concatenate(cols, axis=1) + bs_ref[...])
    s = jnp.concatenate(rows, axis=0)
    out_a = u * s
    ma_ref[...] = _rms(out_a, bga).astype(BF16)
    xb_ref[...] = proj[:, 2 * WIDTH_A:]


def _mix_in_kernel0(x_ref, n1g, sc1, sh1, win_ref, bin_ref, lng, lnb, wm_ref, bs_ref, bga,
                    ma_ref, xb_ref):
    _mix_in_body(x_ref[...], n1g[...], sc1[...], sh1[...], win_ref, bin_ref, lng[...], lnb[...],
                 wm_ref, bs_ref, bga[...], ma_ref, xb_ref)


def _mix_in_kernel1(x_ref, p_ref, g2_ref, n1g, sc1, sh1, win_ref, bin_ref, lng, lnb, wm_ref,
                    bs_ref, bga, xo_ref, ma_ref, xb_ref):
    x = x_ref[...] + g2_ref[...] * p_ref[...]
    xo_ref[...] = x
    _mix_in_body(x, n1g[...], sc1[...], sh1[...], win_ref, bin_ref, lng[...], lnb[...],
                 wm_ref, bs_ref, bga[...], ma_ref, xb_ref)


def _row_spec(width):
    return pl.BlockSpec((1, width), lambda i: (0, 0))


def _tile_spec(tm, width):
    return pl.BlockSpec((tm, width), lambda i: (i, 0))


def _full_spec(shape):
    nd = len(shape)
    return pl.BlockSpec(shape, lambda i: (0,) * nd)


def _layer_spec(shape, layer):
    nd = len(shape)
    return pl.BlockSpec((None,) + tuple(shape), lambda i: (layer,) + (0,) * nd)


def _mix_in_call(x, res, n1g, sc1, sh1, w_in, layer, b_in, lng, lnb, wm, bs_full, bga):
    tm = TM_MIX
    common_specs = [
        _row_spec(D_MODEL), _row_spec(D_MODEL), _row_spec(D_MODEL),
        _layer_spec((D_MODEL, IN_WIDTH), layer), _row_spec(IN_WIDTH),
        _row_spec(WIDTH_A), _row_spec(WIDTH_A),
        _full_spec((GMLP_HEADS, GMLP_BLOCK, GMLP_BLOCK)),
        _full_spec((GMLP_BLOCK, WIDTH_A)),
        _row_spec(WIDTH_A),
    ]
    common_args = (n1g, sc1, sh1, w_in, b_in, lng, lnb, wm, bs_full, bga)
    out_specs = [_tile_spec(tm, WIDTH_A), _tile_spec(tm, WIDTH_B)]
    out_shape = [jax.ShapeDtypeStruct((SEQ, WIDTH_A), BF16),
                 jax.ShapeDtypeStruct((SEQ, WIDTH_B), F32)]
    if res is None:
        ma, xb = pl.pallas_call(
            _mix_in_kernel0, grid=(SEQ // tm,),
            in_specs=[_tile_spec(tm, D_MODEL)] + common_specs,
            out_specs=out_specs, out_shape=out_shape,
            compiler_params=_cparams(("arbitrary",)), name="mix_in",
        )(x, *common_args)
        return x, ma, xb
    peer, g2 = res
    xo, ma, xb = pl.pallas_call(
        _mix_in_kernel1, grid=(SEQ // tm,),
        in_specs=[_tile_spec(tm, D_MODEL), _tile_spec(tm, D_MODEL), _row_spec(D_MODEL)]
        + common_specs,
        out_specs=[_tile_spec(tm, D_MODEL)] + out_specs,
        out_shape=[jax.ShapeDtypeStruct((SEQ, D_MODEL), F32)] + out_shape,
        compiler_params=_cparams(("arbitrary",)), name="mix_in_res",
    )(x, peer, g2, *common_args)
    return xo, ma, xb


def _s5_local_kernel(x_ref, w_ref, o_ref):
    o_ref[...] = jnp.dot(x_ref[0], w_ref[0], preferred_element_type=F32)


def _s5_scan_kernel(l_ref, ca_ref, cb_ref, h_ref):
    ca = ca_ref[...]
    cb = cb_ref[...]
    n_half = ca.shape[1] // LANES

    def step(k, s):
        h_ref[k] = s
        pieces = []
        for j in range(0, n_half, 2):
            pieces.append(s[:, (j + 1) * LANES:(j + 2) * LANES])
            pieces.append(s[:, j * LANES:(j + 1) * LANES])
        sw = jnp.concatenate(pieces, axis=1)
        return ca * s + cb * sw + l_ref[k]

    lax.fori_loop(0, N_CHUNKS, step, jnp.zeros(ca.shape, F32))


def _s5_out_kernel(x_ref, m_ref, h_ref, wo_ref, y_ref):
    y = jnp.dot(x_ref[0], m_ref[0], preferred_element_type=F32)
    h = h_ref[...]
    h_hi = h.astype(BF16)
    h_lo = (h - h_hi.astype(F32)).astype(BF16)
    y += jnp.dot(h_hi, wo_ref[0], preferred_element_type=F32)
    y += jnp.dot(h_lo, wo_ref[0], preferred_element_type=F32)
    y_ref[0] = y


def _s5_call(xb, prep):
    m_toep, w_state, w_out, ca, cb = prep
    g = SSM_GROUPS
    xg = xb.reshape(N_CHUNKS, CHUNK, g, SSM_GROUP).transpose(2, 0, 1, 3)
    xg = xg.reshape(g, N_CHUNKS, CHUNK_W).astype(BF16)
    local = pl.pallas_call(
        _s5_local_kernel, grid=(g,),
        in_specs=[pl.BlockSpec((1, N_CHUNKS, CHUNK_W), lambda i: (i, 0, 0)),
                  pl.BlockSpec((1, CHUNK_W, 4 * SSM_STATE), lambda i: (i, 0, 0))],
        out_specs=pl.BlockSpec((N_CHUNKS, 4 * SSM_STATE), lambda i: (0, i)),
        out_shape=jax.ShapeDtypeStruct((N_CHUNKS, g * 4 * SSM_STATE), F32),
        compiler_params=_cparams(("arbitrary",)), name="s5_local",
    )(xg, w_state)
    rows = 8
    width = g * 4 * SSM_STATE // rows
    hstart = pl.pallas_call(
        _s5_scan_kernel,
        out_shape=jax.ShapeDtypeStruct((N_CHUNKS, rows, width), F32),
        compiler_params=pltpu.CompilerParams(vmem_limit_bytes=VMEM_LIMIT), name="s5_scan",
    )(local.reshape(N_CHUNKS, rows, width), ca, cb)
    hstart = hstart.reshape(N_CHUNKS, g * 4 * SSM_STATE)
    yg = pl.pallas_call(
        _s5_out_kernel, grid=(g,),
        in_specs=[pl.BlockSpec((1, N_CHUNKS, CHUNK_W), lambda i: (i, 0, 0)),
                  pl.BlockSpec((1, CHUNK_W, CHUNK_W), lambda i: (i, 0, 0)),
                  pl.BlockSpec((N_CHUNKS, 2 * SSM_STATE), lambda i: (0, 2 * i)),
                  pl.BlockSpec((1, 2 * SSM_STATE, CHUNK_W), lambda i: (i, 0, 0))],
        out_specs=pl.BlockSpec((1, N_CHUNKS, CHUNK_W), lambda i: (i, 0, 0)),
        out_shape=jax.ShapeDtypeStruct((g, N_CHUNKS, CHUNK_W), F32),
        compiler_params=_cparams(("arbitrary",)), name="s5_out",
    )(xg, m_toep, hstart, w_out)
    y = yg.reshape(g, N_CHUNKS, CHUNK, SSM_GROUP).transpose(1, 2, 0, 3)
    return y.reshape(SEQ, WIDTH_B)


def _s5_prep(lam_re, lam_im, log_dt, b_re, b_im, c_re, c_im):
    hp = lax.Precision.HIGHEST
    cmul = lambda ar, ai, br, bi: (ar * br - ai * bi, ar * bi + ai * br)
    dt = jnp.exp(log_dt)[:, None]
    steps = jnp.arange(0, CHUNK + 1, dtype=F32)[:, None, None]
    mag = jnp.exp(lam_re[None] * (dt[None] * steps))
    ang = lam_im[None] * (dt[None] * steps)
    pw_r, pw_i = mag * jnp.cos(ang), mag * jnp.sin(ang)
    n_r, n_i = pw_r[1] - 1.0, pw_i[1]
    den = lam_re * lam_re + lam_im * lam_im
    q_r = (n_r * lam_re + n_i * lam_im) / den
    q_i = (n_i * lam_re - n_r * lam_im) / den
    bb_r, bb_i = cmul(q_r[..., None], q_i[..., None], b_re, b_im)
    e_r, e_i = cmul(pw_r[:CHUNK, :, :, None], pw_i[:CHUNK, :, :, None], bb_r[None], bb_i[None])
    k_tap = jnp.einsum("ghp,tgpk->tghk", jnp.concatenate([c_re, -c_im], axis=-1),
                       jnp.concatenate([e_r, e_i], axis=2), precision=hp)
    krow = k_tap.transpose(1, 3, 0, 2).reshape(SSM_GROUPS, SSM_GROUP, CHUNK_W).astype(BF16)
    kpad = jnp.concatenate([jnp.zeros_like(krow), krow], axis=-1)
    m_toep = jnp.stack([kpad[:, :, CHUNK_W - SSM_GROUP * c:2 * CHUNK_W - SSM_GROUP * c]
                        for c in range(CHUNK)], axis=1).reshape(SSM_GROUPS, CHUNK_W, CHUNK_W)
    rev_r, rev_i = pw_r[CHUNK - 1::-1][:CHUNK], pw_i[CHUNK - 1::-1][:CHUNK]
    w_r, w_i = cmul(rev_r[..., None], rev_i[..., None], bb_r[None], bb_i[None])
    lay = lambda w: w.transpose(1, 0, 3, 2).reshape(SSM_GROUPS, CHUNK_W, SSM_STATE)
    w_r, w_i = lay(w_r), lay(w_i)
    w_state = jnp.concatenate([w_r, w_i, w_i, w_r], axis=-1).astype(BF16)
    ca_r, ca_i = cmul(c_re[None], c_im[None], pw_r[1:CHUNK + 1, :, None, :],
                      pw_i[1:CHUNK + 1, :, None, :])
    lay = lambda w: w.transpose(1, 3, 0, 2).reshape(SSM_GROUPS, SSM_STATE, CHUNK_W)
    w_out = jnp.concatenate([lay(ca_r), -lay(ca_i)], axis=1).astype(BF16)
    ar, ai = pw_r[CHUNK], pw_i[CHUNK]
    ca = jnp.concatenate([ar, ar, ar, ar], axis=-1).reshape(8, -1)
    cb = jnp.concatenate([-ai, ai, ai, -ai], axis=-1).reshape(8, -1)
    return m_toep, w_state, w_out, ca, cb


def _mix_out_kernel(x_ref, ma_ref, xb_ref, ys_ref, d_ref, gw_ref, gb_ref, bgb_ref, wo_ref,
                    g1_ref, n2g_ref, sc2_ref, sh2_ref, xo_ref, h2_ref):
    y = ys_ref[...] + d_ref[...] * xb_ref[...]
    g = jax.nn.gelu(y)
    gate = jax.nn.sigmoid(jnp.dot(g.astype(BF16), gw_ref[...], preferred_element_type=F32)
                          + gb_ref[...])
    mb = _rms(g * gate, bgb_ref[...]).astype(BF16)
    mo = jnp.dot(ma_ref[...], wo_ref[:WIDTH_A, :], preferred_element_type=F32)
    mo += jnp.dot(mb, wo_ref[WIDTH_A:, :], preferred_element_type=F32)
    x = x_ref[...] + g1_ref[...] * mo
    xo_ref[...] = x
    h2_ref[...] = (_rms(x, n2g_ref[...]) * (1.0 + sc2_ref[...]) + sh2_ref[...]).astype(BF16)


def _mix_out_call(x, ma, xb, ys, d, glu_w, layer, glu_b, bgb, w_out, g1, n2g, sc2, sh2):
    tm = TM_MIX
    return pl.pallas_call(
        _mix_out_kernel, grid=(SEQ // tm,),
        in_specs=[_tile_spec(tm, D_MODEL), _tile_spec(tm, WIDTH_A), _tile_spec(tm, WIDTH_B),
                  _tile_spec(tm, WIDTH_B), _row_spec(WIDTH_B),
                  _layer_spec((WIDTH_B, WIDTH_B), layer), _row_spec(WIDTH_B), _row_spec(WIDTH_B),
                  _layer_spec((D_MODEL, D_MODEL), layer),
                  _row_spec(D_MODEL), _row_spec(D_MODEL), _row_spec(D_MODEL), _row_spec(D_MODEL)],
        out_specs=[_tile_spec(tm, D_MODEL), _tile_spec(tm, D_MODEL)],
        out_shape=[jax.ShapeDtypeStruct((SEQ, D_MODEL), F32),
                   jax.ShapeDtypeStruct((SEQ, D_MODEL), BF16)],
        compiler_params=_cparams(("arbitrary",)), name="mix_out",
    )(x, ma, xb, ys, d, glu_w, glu_b, bgb, w_out, g1, n2g, sc2, sh2)


_SORT_GROUP = 4
_ROW_BITS = 7


def _sorted_groups(x):
    n, w = x.shape
    assert n <= 1 << _ROW_BITS
    base = lax.broadcasted_iota(jnp.int32, (F32_SUBLANES, w), 0)
    groups = []
    for j in range(n // (F32_SUBLANES * _SORT_GROUP)):
        tile = lambda i: (j * _SORT_GROUP + i) * F32_SUBLANES
        v = [x[tile(i):tile(i) + F32_SUBLANES, :] for i in range(_SORT_GROUP)]
        r = [base + tile(i) for i in range(_SORT_GROUP)]

        def exchange(a, b, rows_ordered):
            if rows_ordered:
                first = v[a] >= v[b]
            else:
                first = jnp.logical_or(v[a] > v[b],
                                       jnp.logical_and(v[a] == v[b], r[a] < r[b]))
            v[a], v[b] = jnp.where(first, v[a], v[b]), jnp.where(first, v[b], v[a])
            r[a], r[b] = jnp.where(first, r[a], r[b]), jnp.where(first, r[b], r[a])

        exchange(0, 1, True)
        exchange(2, 3, True)
        exchange(0, 2, True)
        exchange(1, 3, True)
        exchange(1, 2, False)
        packed = r[0]
        for i in range(1, _SORT_GROUP):
            packed = packed | (r[i] << (i * _ROW_BITS))
        groups.append((tuple(v), packed))
    return tuple(groups)


def _topk_rows(xs, k):
    n, w = xs[0].shape
    kiota = lax.broadcasted_iota(jnp.int32, (k, w), 0)
    last = _SORT_GROUP - 1

    def body(it, carry):
        out = []
        for groups, vals, idxs in carry:
            heads = [v[0] for v, _ in groups]
            head_rows = [(rows & ((1 << _ROW_BITS) - 1)).astype(F32) for _, rows in groups]
            m = jnp.max(functools.reduce(jnp.maximum, heads), axis=0, keepdims=True)
            cands = [jnp.where(h == m, hr, float(n)) for h, hr in zip(heads, head_rows)]
            am = jnp.min(functools.reduce(jnp.minimum, cands), axis=0, keepdims=True)
            new_groups = []
            for (v, rows), hr in zip(groups, head_rows):
                won = hr == am
                nv = tuple(jnp.where(won, v[i + 1], v[i]) for i in range(last))
                nv += (jnp.where(won, _NEG_INF, v[last]),)
                new_groups.append((nv, jnp.where(won, rows >> _ROW_BITS, rows)))
            vals = jnp.where(kiota == it, m, vals)
            idxs = jnp.where(kiota == it, am, idxs)
            out.append((tuple(new_groups), vals, idxs))
        return tuple(out)

    zeros = jnp.zeros((k, w), F32)
    res = lax.fori_loop(0, k, body, tuple((_sorted_groups(x), zeros, zeros) for x in xs))
    return [(vals, idxs) for _, vals, idxs in res]


def _topk_pairs(problems, k):
    w = problems[0][0].shape[1]
    kiota = lax.broadcasted_iota(jnp.int32, (k, w), 0)
    aiota = kiota.astype(F32)

    def body(it, carry):
        out = []
        for (front, ptr, vals, ids), (v1, i1, v2, i2) in zip(carry, problems):
            m = jnp.max(front, axis=0, keepdims=True)
            a_sel = jnp.min(jnp.where(front == m, aiota, float(k)), axis=0, keepdims=True)
            won = aiota == a_sel
            b_sel = jnp.max(jnp.where(won, ptr, -1.0), axis=0, keepdims=True)
            e1 = jnp.max(jnp.where(won, i1, -1.0), axis=0, keepdims=True)
            e2 = jnp.max(jnp.where(aiota == b_sel, i2, -1.0), axis=0, keepdims=True)
            v2_next = jnp.max(jnp.where(aiota == b_sel + 1.0, v2, _NEG_INF), axis=0,
                              keepdims=True)
            front = jnp.where(won, v1 + v2_next, front)
            ptr = jnp.where(won, ptr + 1.0, ptr)
            vals = jnp.where(kiota == it, m, vals)
            ids = jnp.where(kiota == it, e1 * float(PEER_KEYS) + e2, ids)
            out.append((front, ptr, vals, ids))
        return tuple(out)

    zeros = jnp.zeros((k, w), F32)
    init = tuple((v1 + v2[0:1, :], zeros, zeros, zeros) for v1, _, v2, _ in problems)
    res = lax.fori_loop(0, k, body, init)
    return [(vals, ids) for _, _, vals, ids in res]


def _peer_topk_kernel(h2_ref, wqt_ref, keys_ref, e_ref, g_ref, qt_ref):
    qt_ref[...] = lax.dot_general(wqt_ref[...], h2_ref[...], (((1,), (1,)), ((), ())),
                                  preferred_element_type=F32).astype(BF16)
    n_tiles = h2_ref.shape[0] // LANES

    def head(hh, _):
        sides = []
        for s in range(2):
            row0 = pl.multiple_of(hh * (2 * PEER_KEYS) + s * PEER_KEYS, PEER_KEYS)
            sides.append(jnp.dot(keys_ref[hh * 2 + s], qt_ref[pl.ds(row0, PEER_KEYS), :],
                                 preferred_element_type=F32))
        out0 = pl.multiple_of(hh * PEER_TOPK, PEER_TOPK)
        tiles = [slice(t * LANES, (t + 1) * LANES) for t in range(n_tiles)]
        tops = _topk_rows([sides[s][:, lanes] for lanes in tiles for s in range(2)], PEER_TOPK)
        problems = [tops[2 * t] + tops[2 * t + 1] for t in range(n_tiles)]
        for lanes, (best, ids) in zip(tiles, _topk_pairs(problems, PEER_TOPK)):
            ex = jnp.exp(best - jnp.max(best, axis=0, keepdims=True))
            gate = ex / jnp.sum(ex, axis=0, keepdims=True)
            e_ref[pl.ds(out0, PEER_TOPK), lanes] = ids.astype(jnp.int32)
            g_ref[pl.ds(out0, PEER_TOPK), lanes] = gate
        return 0

    lax.fori_loop(0, PEER_HEADS, head, 0)


def _peer_topk_call(h2, wq_t, layer, keys):
    tq = TQ_TOPK
    n_slots = PEER_HEADS * PEER_TOPK
    return pl.pallas_call(
        _peer_topk_kernel, grid=(SEQ // tq,),
        in_specs=[_tile_spec(tq, D_MODEL), _layer_spec((D_MODEL, D_MODEL), layer),
                  _full_spec((2 * PEER_HEADS, PEER_KEYS, PEER_KEYS))],
        out_specs=[pl.BlockSpec((n_slots, tq), lambda i: (0, i)),
                   pl.BlockSpec((n_slots, tq), lambda i: (0, i))],
        out_shape=[jax.ShapeDtypeStruct((n_slots, SEQ), jnp.int32),
                   jax.ShapeDtypeStruct((n_slots, SEQ), F32)],
        scratch_shapes=[pltpu.VMEM((D_MODEL, tq), BF16)],
        compiler_params=_cparams(("arbitrary",)), name="peer_topk",
    )(h2, wq_t, keys)


def _peer_gate_kernel(e_ref, g_ref, o_ref, tok_ref):
    tg, n_slots = e_ref.shape
    kio = lax.broadcasted_iota(jnp.int32, (PEER_KEYS, n_slots), 0)

    def token(t, _):
        e = e_ref[pl.ds(t, 1), :]
        gate = g_ref[pl.ds(t, 1), :]
        i1 = e >> 7
        i2 = e & (PEER_KEYS - 1)
        oh1 = jnp.where(i1 == kio, 1.0, 0.0).astype(BF16)
        w2 = jnp.where(i2 == kio, gate, 0.0).astype(BF16)
        row0 = pl.multiple_of(t * GATE_PITCH, F32_SUBLANES)
        tok_ref[pl.ds(row0, PEER_KEYS), :] = lax.dot_general(
            oh1, w2, (((1,), (1,)), ((), ())), preferred_element_type=F32)
        return 0

    lax.fori_loop(0, tg, token, 0, unroll=GATE_UNROLL)

    def plane(k, _):
        o_ref[k] = tok_ref[pl.ds(k, tg, stride=GATE_PITCH), :]
        return 0

    lax.fori_loop(0, PEER_KEYS, plane, 0, unroll=GATE_UNROLL)


def _peer_gate_call(e_rows, g_rows):
    tg = TG_BUILD
    n_slots = PEER_HEADS * PEER_TOPK
    return pl.pallas_call(
        _peer_gate_kernel, grid=(SEQ // tg,),
        in_specs=[_tile_spec(tg, n_slots), _tile_spec(tg, n_slots)],
        out_specs=pl.BlockSpec((PEER_KEYS, tg, PEER_KEYS), lambda i: (0, i, 0)),
        out_shape=jax.ShapeDtypeStruct((PEER_KEYS, SEQ, PEER_KEYS), F32),
        scratch_shapes=[pltpu.VMEM((tg * GATE_PITCH, PEER_KEYS), F32)],
        compiler_params=_cparams(("arbitrary",)), name="peer_gate",
    )(e_rows, g_rows)


def _peer_dense_kernel(h2_ref, u_ref, v_ref, g_ref, o_ref):
    @pl.when(pl.program_id(1) == 0)
    def _():
        o_ref[...] = jnp.zeros_like(o_ref)

    h2 = h2_ref[...]
    keys_per_chunk = EC_DENSE // PEER_KEYS
    n_chunks = EB_DENSE // EC_DENSE
    chunk = lambda c: slice(c * EC_DENSE, (c + 1) * EC_DENSE)
    score = lambda c: lax.dot_general(h2, u_ref[chunk(c), :], (((1,), (1,)), ((), ())),
                                      preferred_element_type=F32)
    a_next = score(0)
    for c in range(n_chunks):
        a = a_next
        if c + 1 < n_chunks:
            a_next = score(c + 1)
        gate = jnp.concatenate([g_ref[c * keys_per_chunk + k] for k in range(keys_per_chunk)],
                               axis=1)
        p = (gate * jax.nn.gelu(a)).astype(BF16)
        o_ref[...] += jnp.dot(p, v_ref[chunk(c), :], preferred_element_type=F32)


def _peer_dense_call(h2, u_tab, v_tab, layer, gmat):
    tb, eb = TB_DENSE, EB_DENSE
    return pl.pallas_call(
        _peer_dense_kernel, grid=(SEQ // tb, PEER_EXPERTS // eb),
        in_specs=[pl.BlockSpec((tb, D_MODEL), lambda i, j: (i, 0)),
                  pl.BlockSpec((None, eb, D_MODEL), lambda i, j: (layer, j, 0)),
                  pl.BlockSpec((None, eb, D_MODEL), lambda i, j: (layer, j, 0)),
                  pl.BlockSpec((KEY1_PER_BLOCK, tb, PEER_KEYS), lambda i, j: (j, i, 0))],
        out_specs=pl.BlockSpec((tb, D_MODEL), lambda i, j: (i, 0)),
        out_shape=jax.ShapeDtypeStruct((SEQ, D_MODEL), F32),
        compiler_params=_cparams(("arbitrary", "arbitrary")), name="peer_dense",
    )(h2, u_tab, v_tab, gmat)


def _final_kernel(x_ref, p_ref, g2_ref, fg_ref, o_ref):
    x = x_ref[...] + g2_ref[...] * p_ref[...]
    o_ref[...] = _rms(x, fg_ref[...])


def _final_call(x, peer, g2, fg):
    tm = TM_FINAL
    return pl.pallas_call(
        _final_kernel, grid=(SEQ // tm,),
        in_specs=[_tile_spec(tm, D_MODEL), _tile_spec(tm, D_MODEL), _row_spec(D_MODEL),
                  _row_spec(D_MODEL)],
        out_specs=_tile_spec(tm, D_MODEL),
        out_shape=jax.ShapeDtypeStruct((SEQ, D_MODEL), F32),
        compiler_params=_cparams(("arbitrary",)), name="final_norm",
    )(x, peer, g2, fg)


def _chunk_causal_mask(n):
    blk = jnp.arange(n) // CHUNK
    return blk[:, None] >= blk[None, :]


def kernel(x, c, w_mod, b_mod, norm1_g, w_in, b_in, gm_ln_g, gm_ln_b, gm_ws, gm_bs, ssm_lam_re, ssm_lam_im, ssm_log_dt, ssm_b_re, ssm_b_im, ssm_c_re, ssm_c_im, ssm_d, glu_w, glu_b, branch_g, w_out, norm2_g, peer_wq, peer_keys, peer_u, peer_v, final_g):
    assert x.shape == (1, SEQ, D_MODEL)
    xt = x.reshape(SEQ, D_MODEL)
    mod = _mod_call(c, w_mod, b_mod).reshape(DEPTH, N_MOD, 1, D_MODEL)
    row = lambda v: v.reshape(1, -1)
    res = None
    w_in_b, glu_w_b, w_out_b = w_in.astype(BF16), glu_w.astype(BF16), w_out.astype(BF16)
    wq_t_b = peer_wq.astype(BF16).transpose(0, 2, 1)
    u_b, v_b = peer_u.astype(BF16), peer_v.astype(BF16)
    for i in range(DEPTH):
        sh1, sc1, g1, sh2, sc2, g2 = (mod[i, k] for k in range(N_MOD))
        wm = jnp.where(_chunk_causal_mask(GMLP_BLOCK)[None], gm_ws[i], 0.0).astype(BF16)
        bs_full = jnp.repeat(gm_bs[i].T, LANES, axis=1)
        xt, ma, xb = _mix_in_call(
            xt, res, row(norm1_g[i]), sc1, sh1, w_in_b, i, row(b_in[i]),
            row(gm_ln_g[i]), row(gm_ln_b[i]), wm, bs_full, row(branch_g[i, :WIDTH_A]))
        prep = _s5_prep(ssm_lam_re[i], ssm_lam_im[i], ssm_log_dt[i], ssm_b_re[i], ssm_b_im[i],
                        ssm_c_re[i], ssm_c_im[i])
        ys = _s5_call(xb, prep)
        xt, h2 = _mix_out_call(
            xt, ma, xb, ys, row(ssm_d[i]), glu_w_b, i, row(glu_b[i]),
            row(branch_g[i, WIDTH_A:]), w_out_b, g1, row(norm2_g[i]), sc2, sh2)
        keys = peer_keys[i].reshape(2 * PEER_HEADS, PEER_KEYS, PEER_KEYS).astype(BF16)
        e_t, gate_t = _peer_topk_call(h2, wq_t_b, i, keys)
        gmat = _peer_gate_call(e_t.T, gate_t.T)
        peer = _peer_dense_call(h2, u_b, v_b, i, gmat)
        res = (peer, g2)
    out = _final_call(xt, res[0], res[1], row(final_g))
    return out.reshape(1, SEQ, D_MODEL)
```

```python
import functools
import math

import jax
import jax.numpy as jnp
from jax import lax
from jax.experimental import pallas as pl
from jax.experimental.pallas import tpu as pltpu

F32 = jnp.float32
BF16 = jnp.bfloat16

D_MODEL = 2048
SEQ = 16384
DEPTH = 2
CHUNK = 64
WIDTH_A = 1024
WIDTH_B = 1024
IN_WIDTH = 3072
GMLP_BLOCK = 128
GMLP_HEADS = 8
SSM_GROUP = 16
SSM_GROUPS = 64
SSM_STATE = 64
PEER_HEADS = 8
PEER_KEYS = 128
PEER_EXPERTS = PEER_KEYS * PEER_KEYS
PEER_TOPK = 16
N_MOD = 6
EPS = 1e-6

LANES = 128
N_CHUNKS = SEQ // CHUNK
CHUNK_W = CHUNK * SSM_GROUP
VMEM_LIMIT = 56 * 1024 * 1024

TM_MIX = 512
TQ_TOPK = 256
TG_BUILD = 64
GATE_UNROLL = 64
TB_DENSE = 1024
EB_DENSE = 1024
EC_DENSE = 512
KEY1_PER_BLOCK = EB_DENSE // PEER_KEYS
F32_SUBLANES = 8
GATE_PITCH = PEER_KEYS + F32_SUBLANES
TM_FINAL = 512
TN_MOD = 1024

_NEG_INF = float("-inf")


def _cparams(sem):
    return pltpu.CompilerParams(dimension_semantics=sem, vmem_limit_bytes=VMEM_LIMIT)


def _rms(x, g):
    return x * lax.rsqrt(jnp.mean(x * x, axis=-1, keepdims=True) + EPS) * g


def _mod_kernel(c_ref, w_ref, b_ref, o_ref):
    c = c_ref[...]
    cond = c / (1.0 + jnp.exp(-c))
    w = w_ref[0]
    o_ref[0] = jnp.sum(w * cond, axis=0, keepdims=True) + b_ref[0]


def _mod_call(c, w_mod, b_mod):
    n = N_MOD * D_MODEL
    return pl.pallas_call(
        _mod_kernel,
        grid=(DEPTH, n // TN_MOD),
        in_specs=[
            pl.BlockSpec((D_MODEL, 1), lambda l, j: (0, 0)),
            pl.BlockSpec((1, D_MODEL, TN_MOD), lambda l, j: (l, 0, j)),
            pl.BlockSpec((1, 1, TN_MOD), lambda l, j: (l, 0, j)),
        ],
        out_specs=pl.BlockSpec((1, 1, TN_MOD), lambda l, j: (l, 0, j)),
        out_shape=jax.ShapeDtypeStruct((DEPTH, 1, n), F32),
        compiler_params=_cparams(("arbitrary", "arbitrary")),
        name="mod_gemv",
    )(c.reshape(D_MODEL, 1), w_mod, b_mod.reshape(DEPTH, 1, n))


def _mix_in_body(x, n1g, sc1, sh1, win_ref, bin_ref, lng, lnb, wm_ref, bs_ref, bga,
                 ma_ref, xb_ref):
    h = (_rms(x, n1g) * (1.0 + sc1) + sh1).astype(BF16)
    proj = jnp.dot(h, win_ref[...], preferred_element_type=F32) + bin_ref[...]
    z = jax.nn.gelu(proj[:, :2 * WIDTH_A])
    u = z[:, :WIDTH_A]
    v = z[:, WIDTH_A:]
    mu = jnp.mean(v, axis=-1, keepdims=True)
    vc = v - mu
    var = jnp.mean(vc * vc, axis=-1, keepdims=True)
    vn = (vc * lax.rsqrt(var + EPS) * lng + lnb).astype(BF16)
    tm = x.shape[0]
    rows = []
    for n in range(tm // GMLP_BLOCK):
        vb = vn[n * GMLP_BLOCK:(n + 1) * GMLP_BLOCK, :]
        cols = []
        for hh in range(GMLP_HEADS):
            cols.append(jnp.dot(wm_ref[hh], vb[:, hh * LANES:(hh + 1) * LANES],
                                preferred_element_type=F32))
        rows.append(jnp.concatenate(cols, axis=1) + bs_ref[...])
    s = jnp.concatenate(rows, axis=0)
    out_a = u * s
    ma_ref[...] = _rms(out_a, bga).astype(BF16)
    xb_ref[...] = proj[:, 2 * WIDTH_A:]


def _mix_in_kernel0(x_ref, n1g, sc1, sh1, win_ref, bin_ref, lng, lnb, wm_ref, bs_ref, bga,
                    ma_ref, xb_ref):
    _mix_in_body(x_ref[...], n1g[...], sc1[...], sh1[...], win_ref, bin_ref, lng[...], lnb[...],
                 wm_ref, bs_ref, bga[...], ma_ref, xb_ref)


def _mix_in_kernel1(x_ref, p_ref, g2_ref, n1g, sc1, sh1, win_ref, bin_ref, lng, lnb, wm_ref,
                    bs_ref, bga, xo_ref, ma_ref, xb_ref):
    x = x_ref[...] + g2_ref[...] * p_ref[...]
    xo_ref[...] = x
    _mix_in_body(x, n1g[...], sc1[...], sh1[...], win_ref, bin_ref, lng[...], lnb[...],
                 wm_ref, bs_ref, bga[...], ma_ref, xb_ref)


def _row_spec(width):
    return pl.BlockSpec((1, width), lambda i: (0, 0))


def _tile_spec(tm, width):
    return pl.BlockSpec((tm, width), lambda i: (i, 0))


def _full_spec(shape):
    nd = len(shape)
    return pl.BlockSpec(shape, lambda i: (0,) * nd)


def _layer_spec(shape, layer):
    nd = len(shape)
    return pl.BlockSpec((None,) + tuple(shape), lambda i: (layer,) + (0,) * nd,
                        pipeline_mode=pl.Buffered(1))


def _mix_in_call(x, res, n1g, sc1, sh1, w_in, layer, b_in, lng, lnb, wm, bs_full, bga):
    tm = TM_MIX
    common_specs = [
        _row_spec(D_MODEL), _row_spec(D_MODEL), _row_spec(D_MODEL),
        _layer_spec((D_MODEL, IN_WIDTH), layer), _row_spec(IN_WIDTH),
        _row_spec(WIDTH_A), _row_spec(WIDTH_A),
        _full_spec((GMLP_HEADS, GMLP_BLOCK, GMLP_BLOCK)),
        _full_spec((GMLP_BLOCK, WIDTH_A)),
        _row_spec(WIDTH_A),
    ]
    common_args = (n1g, sc1, sh1, w_in, b_in, lng, lnb, wm, bs_full, bga)
    out_specs = [_tile_spec(tm, WIDTH_A), _tile_spec(tm, WIDTH_B)]
    out_shape = [jax.ShapeDtypeStruct((SEQ, WIDTH_A), BF16),
                 jax.ShapeDtypeStruct((SEQ, WIDTH_B), F32)]
    if res is None:
        ma, xb = pl.pallas_call(
            _mix_in_kernel0, grid=(SEQ // tm,),
            in_specs=[_tile_spec(tm, D_MODEL)] + common_specs,
            out_specs=out_specs, out_shape=out_shape,
            compiler_params=_cparams(("arbitrary",)), name="mix_in",
        )(x, *common_args)
        return x, ma, xb
    peer, g2 = res
    xo, ma, xb = pl.pallas_call(
        _mix_in_kernel1, grid=(SEQ // tm,),
        in_specs=[_tile_spec(tm, D_MODEL), _tile_spec(tm, D_MODEL), _row_spec(D_MODEL)]
        + common_specs,
        out_specs=[_tile_spec(tm, D_MODEL)] + out_specs,
        out_shape=[jax.ShapeDtypeStruct((SEQ, D_MODEL), F32)] + out_shape,
        compiler_params=_cparams(("arbitrary",)), name="mix_in_res",
    )(x, peer, g2, *common_args)
    return xo, ma, xb


def _s5_local_kernel(x_ref, w_ref, o_ref):
    o_ref[...] = jnp.dot(x_ref[0], w_ref[0], preferred_element_type=F32)


def _s5_scan_kernel(l_ref, ca_ref, cb_ref, h_ref):
    ca = ca_ref[...]
    cb = cb_ref[...]
    n_half = ca.shape[1] // LANES

    def step(k, s):
        h_ref[k] = s
        pieces = []
        for j in range(0, n_half, 2):
            pieces.append(s[:, (j + 1) * LANES:(j + 2) * LANES])
            pieces.append(s[:, j * LANES:(j + 1) * LANES])
        sw = jnp.concatenate(pieces, axis=1)
        return ca * s + cb * sw + l_ref[k]

    lax.fori_loop(0, N_CHUNKS, step, jnp.zeros(ca.shape, F32))


def _s5_out_kernel(x_ref, m_ref, h_ref, wo_ref, y_ref):
    y = jnp.dot(x_ref[0], m_ref[0], preferred_element_type=F32)
    h = h_ref[...]
    h_hi = h.astype(BF16)
    h_lo = (h - h_hi.astype(F32)).astype(BF16)
    y += jnp.dot(h_hi, wo_ref[0], preferred_element_type=F32)
    y += jnp.dot(h_lo, wo_ref[0], preferred_element_type=F32)
    y_ref[0] = y.astype(y_ref.dtype)


def _s5_call(xb, prep):
    m_toep, w_state, w_out, ca, cb = prep
    g = SSM_GROUPS
    xg = xb.reshape(N_CHUNKS, CHUNK, g, SSM_GROUP).transpose(2, 0, 1, 3)
    xg = xg.reshape(g, N_CHUNKS, CHUNK_W).astype(BF16)
    local = pl.pallas_call(
        _s5_local_kernel, grid=(g,),
        in_specs=[pl.BlockSpec((1, N_CHUNKS, CHUNK_W), lambda i: (i, 0, 0)),
                  pl.BlockSpec((1, CHUNK_W, 4 * SSM_STATE), lambda i: (i, 0, 0))],
        out_specs=pl.BlockSpec((N_CHUNKS, 4 * SSM_STATE), lambda i: (0, i)),
        out_shape=jax.ShapeDtypeStruct((N_CHUNKS, g * 4 * SSM_STATE), F32),
        compiler_params=_cparams(("arbitrary",)), name="s5_local",
    )(xg, w_state)
    rows = 8
    width = g * 4 * SSM_STATE // rows
    hstart = pl.pallas_call(
        _s5_scan_kernel,
        out_shape=jax.ShapeDtypeStruct((N_CHUNKS, rows, width), F32),
        compiler_params=pltpu.CompilerParams(vmem_limit_bytes=VMEM_LIMIT), name="s5_scan",
    )(local.reshape(N_CHUNKS, rows, width), ca, cb)
    hstart = hstart.reshape(N_CHUNKS, g * 4 * SSM_STATE)
    yg = pl.pallas_call(
        _s5_out_kernel, grid=(g,),
        in_specs=[pl.BlockSpec((1, N_CHUNKS, CHUNK_W), lambda i: (i, 0, 0)),
                  pl.BlockSpec((1, CHUNK_W, CHUNK_W), lambda i: (i, 0, 0)),
                  pl.BlockSpec((N_CHUNKS, 2 * SSM_STATE), lambda i: (0, 2 * i)),
                  pl.BlockSpec((1, 2 * SSM_STATE, CHUNK_W), lambda i: (i, 0, 0))],
        out_specs=pl.BlockSpec((1, N_CHUNKS, CHUNK_W), lambda i: (i, 0, 0)),
        out_shape=jax.ShapeDtypeStruct((g, N_CHUNKS, CHUNK_W), BF16),
        compiler_params=_cparams(("arbitrary",)), name="s5_out",
    )(xg, m_toep, hstart, w_out)
    y = yg.reshape(g, N_CHUNKS, CHUNK, SSM_GROUP).transpose(1, 2, 0, 3)
    return y.reshape(SEQ, WIDTH_B)


def _s5_prep(lam_re, lam_im, log_dt, b_re, b_im, c_re, c_im):
    hp = lax.Precision.HIGHEST
    cmul = lambda ar, ai, br, bi: (ar * br - ai * bi, ar * bi + ai * br)
    dt = jnp.exp(log_dt)[:, None]
    steps = jnp.arange(0, CHUNK + 1, dtype=F32)[:, None, None]
    mag = jnp.exp(lam_re[None] * (dt[None] * steps))
    ang = lam_im[None] * (dt[None] * steps)
    pw_r, pw_i = mag * jnp.cos(ang), mag * jnp.sin(ang)
    n_r, n_i = pw_r[1] - 1.0, pw_i[1]
    den = lam_re * lam_re + lam_im * lam_im
    q_r = (n_r * lam_re + n_i * lam_im) / den
    q_i = (n_i * lam_re - n_r * lam_im) / den
    bb_r, bb_i = cmul(q_r[..., None], q_i[..., None], b_re, b_im)
    e_r, e_i = cmul(pw_r[:CHUNK, :, :, None], pw_i[:CHUNK, :, :, None], bb_r[None], bb_i[None])
    k_tap = jnp.einsum("ghp,tgpk->tghk", jnp.concatenate([c_re, -c_im], axis=-1),
                       jnp.concatenate([e_r, e_i], axis=2), precision=hp)
    krow = k_tap.transpose(1, 3, 0, 2).reshape(SSM_GROUPS, SSM_GROUP, CHUNK_W).astype(BF16)
    kpad = jnp.concatenate([jnp.zeros_like(krow), krow], axis=-1)
    m_toep = jnp.stack([kpad[:, :, CHUNK_W - SSM_GROUP * c:2 * CHUNK_W - SSM_GROUP * c]
                        for c in range(CHUNK)], axis=1).reshape(SSM_GROUPS, CHUNK_W, CHUNK_W)
    rev_r, rev_i = pw_r[CHUNK - 1::-1][:CHUNK], pw_i[CHUNK - 1::-1][:CHUNK]
    w_r, w_i = cmul(rev_r[..., None], rev_i[..., None], bb_r[None], bb_i[None])
    lay = lambda w: w.transpose(1, 0, 3, 2).reshape(SSM_GROUPS, CHUNK_W, SSM_STATE)
    w_r, w_i = lay(w_r), lay(w_i)
    w_state = jnp.concatenate([w_r, w_i, w_i, w_r], axis=-1).astype(BF16)
    ca_r, ca_i = cmul(c_re[None], c_im[None], pw_r[1:CHUNK + 1, :, None, :],
                      pw_i[1:CHUNK + 1, :, None, :])
    lay = lambda w: w.transpose(1, 3, 0, 2).reshape(SSM_GROUPS, SSM_STATE, CHUNK_W)
    w_out = jnp.concatenate([lay(ca_r), -lay(ca_i)], axis=1).astype(BF16)
    ar, ai = pw_r[CHUNK], pw_i[CHUNK]
    ca = jnp.concatenate([ar, ar, ar, ar], axis=-1).reshape(8, -1)
    cb = jnp.concatenate([-ai, ai, ai, -ai], axis=-1).reshape(8, -1)
    return m_toep, w_state, w_out, ca, cb


def _mix_out_kernel(x_ref, ma_ref, xb_ref, ys_ref, d_ref, gw_ref, gb_ref, bgb_ref, wo_ref,
                    g1_ref, n2g_ref, sc2_ref, sh2_ref, xo_ref, h2_ref):
    y = ys_ref[...].astype(F32) + d_ref[...] * xb_ref[...]
    g = jax.nn.gelu(y)
    gate = jax.nn.sigmoid(jnp.dot(g.astype(BF16), gw_ref[...], preferred_element_type=F32)
                          + gb_ref[...])
    mb = _rms(g * gate, bgb_ref[...]).astype(BF16)
    mo = jnp.dot(ma_ref[...], wo_ref[:WIDTH_A, :], preferred_element_type=F32)
    mo += jnp.dot(mb, wo_ref[WIDTH_A:, :], preferred_element_type=F32)
    x = x_ref[...] + g1_ref[...] * mo
    xo_ref[...] = x
    h2_ref[...] = (_rms(x, n2g_ref[...]) * (1.0 + sc2_ref[...]) + sh2_ref[...]).astype(BF16)


def _mix_out_call(x, ma, xb, ys, d, glu_w, layer, glu_b, bgb, w_out, g1, n2g, sc2, sh2):
    tm = TM_MIX
    return pl.pallas_call(
        _mix_out_kernel, grid=(SEQ // tm,),
        in_specs=[_tile_spec(tm, D_MODEL), _tile_spec(tm, WIDTH_A), _tile_spec(tm, WIDTH_B),
                  _tile_spec(tm, WIDTH_B), _row_spec(WIDTH_B),
                  _layer_spec((WIDTH_B, WIDTH_B), layer), _row_spec(WIDTH_B), _row_spec(WIDTH_B),
                  _layer_spec((D_MODEL, D_MODEL), layer),
                  _row_spec(D_MODEL), _row_spec(D_MODEL), _row_spec(D_MODEL), _row_spec(D_MODEL)],
        out_specs=[_tile_spec(tm, D_MODEL), _tile_spec(tm, D_MODEL)],
        out_shape=[jax.ShapeDtypeStruct((SEQ, D_MODEL), F32),
                   jax.ShapeDtypeStruct((SEQ, D_MODEL), BF16)],
        compiler_params=_cparams(("arbitrary",)), name="mix_out",
    )(x, ma, xb, ys, d, glu_w, glu_b, bgb, w_out, g1, n2g, sc2, sh2)


_SORT_GROUP = 4
_ROW_BITS = 7


def _sorted_groups(x):
    n, w = x.shape
    assert n <= 1 << _ROW_BITS
    base = lax.broadcasted_iota(jnp.int32, (F32_SUBLANES, w), 0)
    groups = []
    for j in range(n // (F32_SUBLANES * _SORT_GROUP)):
        tile = lambda i: (j * _SORT_GROUP + i) * F32_SUBLANES
        v = [x[tile(i):tile(i) + F32_SUBLANES, :] for i in range(_SORT_GROUP)]
        r = [base + tile(i) for i in range(_SORT_GROUP)]

        def exchange(a, b, rows_ordered):
            if rows_ordered:
                first = v[a] >= v[b]
            else:
                first = jnp.logical_or(v[a] > v[b],
                                       jnp.logical_and(v[a] == v[b], r[a] < r[b]))
            v[a], v[b] = jnp.where(first, v[a], v[b]), jnp.where(first, v[b], v[a])
            r[a], r[b] = jnp.where(first, r[a], r[b]), jnp.where(first, r[b], r[a])

        exchange(0, 1, True)
        exchange(2, 3, True)
        exchange(0, 2, True)
        exchange(1, 3, True)
        exchange(1, 2, False)
        packed = r[0]
        for i in range(1, _SORT_GROUP):
            packed = packed | (r[i] << (i * _ROW_BITS))
        groups.append((tuple(v), packed))
    return tuple(groups)


def _rows_step(it, state, n, kiota):
    groups, vals, idxs = state
    last = _SORT_GROUP - 1
    heads = [v[0] for v, _ in groups]
    head_rows = [(rows & ((1 << _ROW_BITS) - 1)).astype(F32) for _, rows in groups]
    m = jnp.max(functools.reduce(jnp.maximum, heads), axis=0, keepdims=True)
    cands = [jnp.where(h == m, hr, float(n)) for h, hr in zip(heads, head_rows)]
    am = jnp.min(functools.reduce(jnp.minimum, cands), axis=0, keepdims=True)
    new_groups = []
    for (v, rows), hr in zip(groups, head_rows):
        won = hr == am
        nv = tuple(jnp.where(won, v[i + 1], v[i]) for i in range(last))
        nv += (jnp.where(won, _NEG_INF, v[last]),)
        new_groups.append((nv, jnp.where(won, rows >> _ROW_BITS, rows)))
    return (tuple(new_groups), jnp.where(kiota == it, m, vals), jnp.where(kiota == it, am, idxs))


def _pairs_step(it, state, lists, kiota, aiota):
    front, ptr, vals, ids = state
    v1, i1, v2, i2 = lists
    k = front.shape[0]
    m = jnp.max(front, axis=0, keepdims=True)
    a_sel = jnp.min(jnp.where(front == m, aiota, float(k)), axis=0, keepdims=True)
    won = aiota == a_sel
    b_sel = jnp.max(jnp.where(won, ptr, -1.0), axis=0, keepdims=True)
    e1 = jnp.max(jnp.where(won, i1, -1.0), axis=0, keepdims=True)
    e2 = jnp.max(jnp.where(aiota == b_sel, i2, -1.0), axis=0, keepdims=True)
    v2_next = jnp.max(jnp.where(aiota == b_sel + 1.0, v2, _NEG_INF), axis=0,
                      keepdims=True)
    return (jnp.where(won, v1 + v2_next, front), jnp.where(won, ptr + 1.0, ptr),
            jnp.where(kiota == it, m, vals),
            jnp.where(kiota == it, e1 * float(PEER_KEYS) + e2, ids))


def _select(xs, problems, k):
    n = xs[0].shape[0] if xs else 0
    w = (xs[0] if xs else problems[0][0]).shape[1]
    kiota = lax.broadcasted_iota(jnp.int32, (k, w), 0)
    aiota = kiota.astype(F32)

    def body(it, carry):
        rows_states, pair_states = carry
        return (tuple(_rows_step(it, s, n, kiota) for s in rows_states),
                tuple(_pairs_step(it, s, p, kiota, aiota) for s, p in zip(pair_states, problems)))

    zeros = jnp.zeros((k, w), F32)
    init = (tuple((_sorted_groups(x), zeros, zeros) for x in xs),
            tuple((v1 + v2[0:1, :], zeros, zeros, zeros) for v1, _, v2, _ in problems))
    rows_res, pair_res = lax.fori_loop(0, k, body, init)
    return ([(vals, idxs) for _, vals, idxs in rows_res],
            [(vals, ids) for _, _, vals, ids in pair_res])


def _peer_topk_kernel(h2_ref, wqt_ref, keys_ref, e_ref, g_ref, qt_ref):
    qt_ref[...] = lax.dot_general(wqt_ref[...], h2_ref[...], (((1,), (1,)), ((), ())),
                                  preferred_element_type=F32).astype(BF16)
    n_tiles = h2_ref.shape[0] // LANES
    tiles = [slice(t * LANES, (t + 1) * LANES) for t in range(n_tiles)]

    def emit(hh, pair_res):
        out0 = pl.multiple_of(hh * PEER_TOPK, PEER_TOPK)
        for lanes, (best, ids) in zip(tiles, pair_res):
            ex = jnp.exp(best - jnp.max(best, axis=0, keepdims=True))
            gate = ex / jnp.sum(ex, axis=0, keepdims=True)
            e_ref[pl.ds(out0, PEER_TOPK), lanes] = ids.astype(jnp.int32)
            g_ref[pl.ds(out0, PEER_TOPK), lanes] = gate

    def head(hh, _):
        sides = []
        for s in range(2):
            row0 = pl.multiple_of(hh * (2 * PEER_KEYS) + s * PEER_KEYS, PEER_KEYS)
            sides.append(jnp.dot(keys_ref[hh * 2 + s], qt_ref[pl.ds(row0, PEER_KEYS), :],
                                 preferred_element_type=F32))
        problems = []
        for lanes in tiles:
            (top1, top2), _ = _select([sides[0][:, lanes], sides[1][:, lanes]], [], PEER_TOPK)
            problems.append(top1 + top2)
        emit(hh, _select([], problems, PEER_TOPK)[1])
        return 0

    lax.fori_loop(0, PEER_HEADS, head, 0)


def _peer_topk_call(h2, wq_t, layer, keys):
    tq = TQ_TOPK
    n_slots = PEER_HEADS * PEER_TOPK
    return pl.pallas_call(
        _peer_topk_kernel, grid=(SEQ // tq,),
        in_specs=[_tile_spec(tq, D_MODEL), _layer_spec((D_MODEL, D_MODEL), layer),
                  _full_spec((2 * PEER_HEADS, PEER_KEYS, PEER_KEYS))],
        out_specs=[pl.BlockSpec((n_slots, tq), lambda i: (0, i)),
                   pl.BlockSpec((n_slots, tq), lambda i: (0, i))],
        out_shape=[jax.ShapeDtypeStruct((n_slots, SEQ), jnp.int32),
                   jax.ShapeDtypeStruct((n_slots, SEQ), F32)],
        scratch_shapes=[pltpu.VMEM((D_MODEL, tq), BF16)],
        compiler_params=_cparams(("arbitrary",)), name="peer_topk",
    )(h2, wq_t, keys)


def _peer_gate_kernel(e_ref, g_ref, o_ref, tok_ref):
    tg, n_slots = e_ref.shape
    kio = lax.broadcasted_iota(jnp.int32, (PEER_KEYS, n_slots), 0)

    def token(t, _):
        e = e_ref[pl.ds(t, 1), :]
        gate = g_ref[pl.ds(t, 1), :]
        i1 = e >> 7
        i2 = e & (PEER_KEYS - 1)
        oh1 = jnp.where(i1 == kio, 1.0, 0.0).astype(BF16)
        w2 = jnp.where(i2 == kio, gate, 0.0).astype(BF16)
        row0 = pl.multiple_of(t * GATE_PITCH, F32_SUBLANES)
        tok_ref[pl.ds(row0, PEER_KEYS), :] = lax.dot_general(
            oh1, w2, (((1,), (1,)), ((), ())), preferred_element_type=F32)
        return 0

    lax.fori_loop(0, tg, token, 0, unroll=GATE_UNROLL)

    def plane(k, _):
        o_ref[k] = tok_ref[pl.ds(k, tg, stride=GATE_PITCH), :]
        return 0

    lax.fori_loop(0, PEER_KEYS, plane, 0, unroll=GATE_UNROLL)


def _peer_gate_call(e_rows, g_rows):
    tg = TG_BUILD
    n_slots = PEER_HEADS * PEER_TOPK
    return pl.pallas_call(
        _peer_gate_kernel, grid=(SEQ // tg,),
        in_specs=[_tile_spec(tg, n_slots), _tile_spec(tg, n_slots)],
        out_specs=pl.BlockSpec((PEER_KEYS, tg, PEER_KEYS), lambda i: (0, i, 0)),
        out_shape=jax.ShapeDtypeStruct((PEER_KEYS, SEQ, PEER_KEYS), F32),
        scratch_shapes=[pltpu.VMEM((tg * GATE_PITCH, PEER_KEYS), F32)],
        compiler_params=_cparams(("arbitrary",)), name="peer_gate",
    )(e_rows, g_rows)


def _peer_dense_kernel(h2_ref, u_ref, v_ref, g_ref, o_ref):
    @pl.when(pl.program_id(1) == 0)
    def _():
        o_ref[...] = jnp.zeros_like(o_ref)

    h2 = h2_ref[...]
    keys_per_chunk = EC_DENSE // PEER_KEYS
    n_chunks = EB_DENSE // EC_DENSE
    chunk = lambda c: slice(c * EC_DENSE, (c + 1) * EC_DENSE)
    score = lambda c: lax.dot_general(h2, u_ref[chunk(c), :], (((1,), (1,)), ((), ())),
                                      preferred_element_type=F32)
    a_next = score(0)
    for c in range(n_chunks):
        a = a_next
        if c + 1 < n_chunks:
            a_next = score(c + 1)
        gate = jnp.concatenate([g_ref[c * keys_per_chunk + k] for k in range(keys_per_chunk)],
                               axis=1)
        p = (gate * jax.nn.gelu(a)).astype(BF16)
        o_ref[...] += jnp.dot(p, v_ref[chunk(c), :], preferred_element_type=F32)


def _peer_dense_call(h2, u_tab, v_tab, layer, gmat):
    tb, eb = TB_DENSE, EB_DENSE
    return pl.pallas_call(
        _peer_dense_kernel, grid=(SEQ // tb, PEER_EXPERTS // eb),
        in_specs=[pl.BlockSpec((tb, D_MODEL), lambda i, j: (i, 0)),
                  pl.BlockSpec((None, eb, D_MODEL), lambda i, j: (layer, j, 0)),
                  pl.BlockSpec((None, eb, D_MODEL), lambda i, j: (layer, j, 0)),
                  pl.BlockSpec((KEY1_PER_BLOCK, tb, PEER_KEYS), lambda i, j: (j, i, 0))],
        out_specs=pl.BlockSpec((tb, D_MODEL), lambda i, j: (i, 0)),
        out_shape=jax.ShapeDtypeStruct((SEQ, D_MODEL), F32),
        compiler_params=_cparams(("arbitrary", "arbitrary")), name="peer_dense",
    )(h2, u_tab, v_tab, gmat)


def _final_kernel(x_ref, p_ref, g2_ref, fg_ref, o_ref):
    x = x_ref[...] + g2_ref[...] * p_ref[...]
    o_ref[...] = _rms(x, fg_ref[...])


def _final_call(x, peer, g2, fg):
    tm = TM_FINAL
    return pl.pallas_call(
        _final_kernel, grid=(SEQ // tm,),
        in_specs=[_tile_spec(tm, D_MODEL), _tile_spec(tm, D_MODEL), _row_spec(D_MODEL),
                  _row_spec(D_MODEL)],
        out_specs=_tile_spec(tm, D_MODEL),
        out_shape=jax.ShapeDtypeStruct((SEQ, D_MODEL), F32),
        compiler_params=_cparams(("arbitrary",)), name="final_norm",
    )(x, peer, g2, fg)


def _chunk_causal_mask(n):
    blk = jnp.arange(n) // CHUNK
    return blk[:, None] >= blk[None, :]


def kernel(x, c, w_mod, b_mod, norm1_g, w_in, b_in, gm_ln_g, gm_ln_b, gm_ws, gm_bs, ssm_lam_re, ssm_lam_im, ssm_log_dt, ssm_b_re, ssm_b_im, ssm_c_re, ssm_c_im, ssm_d, glu_w, glu_b, branch_g, w_out, norm2_g, peer_wq, peer_keys, peer_u, peer_v, final_g):
    assert x.shape == (1, SEQ, D_MODEL)
    xt = x.reshape(SEQ, D_MODEL)
    mod = _mod_call(c, w_mod, b_mod).reshape(DEPTH, N_MOD, 1, D_MODEL)
    row = lambda v: v.reshape(1, -1)
    res = None
    w_in_b, glu_w_b, w_out_b = w_in.astype(BF16), glu_w.astype(BF16), w_out.astype(BF16)
    wq_t_b = peer_wq.astype(BF16).transpose(0, 2, 1)
    u_b, v_b = peer_u.astype(BF16), peer_v.astype(BF16)
    for i in range(DEPTH):
        sh1, sc1, g1, sh2, sc2, g2 = (mod[i, k] for k in range(N_MOD))
        wm = jnp.where(_chunk_causal_mask(GMLP_BLOCK)[None], gm_ws[i], 0.0).astype(BF16)
        bs_full = jnp.repeat(gm_bs[i].T, LANES, axis=1)
        xt, ma, xb = _mix_in_call(
            xt, res, row(norm1_g[i]), sc1, sh1, w_in_b, i, row(b_in[i]),
            row(gm_ln_g[i]), row(gm_ln_b[i]), wm, bs_full, row(branch_g[i, :WIDTH_A]))
        prep = _s5_prep(ssm_lam_re[i], ssm_lam_im[i], ssm_log_dt[i], ssm_b_re[i], ssm_b_im[i],
                        ssm_c_re[i], ssm_c_im[i])
        ys = _s5_call(xb, prep)
        xt, h2 = _mix_out_call(
            xt, ma, xb, ys, row(ssm_d[i]), glu_w_b, i, row(glu_b[i]),
            row(branch_g[i, WIDTH_A:]), w_out_b, g1, row(norm2_g[i]), sc2, sh2)
        keys = peer_keys[i].reshape(2 * PEER_HEADS, PEER_KEYS, PEER_KEYS).astype(BF16)
        e_t, gate_t = _peer_topk_call(h2, wq_t_b, i, keys)
        gmat = _peer_gate_call(e_t.T, gate_t.T)
        peer = _peer_dense_call(h2, u_b, v_b, i, gmat)
        res = (peer, g2)
    out = _final_call(xt, res[0], res[1], row(final_g))
    return out.reshape(1, SEQ, D_MODEL)
```

```python
import functools
import math

import jax
import jax.numpy as jnp
from jax import lax
from jax.experimental import pallas as pl
from jax.experimental.pallas import tpu as pltpu

F32 = jnp.float32
BF16 = jnp.bfloat16

D_MODEL = 2048
SEQ = 16384
DEPTH = 2
CHUNK = 64
WIDTH_A = 1024
WIDTH_B = 1024
IN_WIDTH = 3072
GMLP_BLOCK = 128
GMLP_HEADS = 8
SSM_GROUP = 16
SSM_GROUPS = 64
SSM_STATE = 64
PEER_HEADS = 8
PEER_KEYS = 128
PEER_EXPERTS = PEER_KEYS * PEER_KEYS
PEER_TOPK = 16
N_MOD = 6
EPS = 1e-6

LANES = 128
N_CHUNKS = SEQ // CHUNK
CHUNK_W = CHUNK * SSM_GROUP
VMEM_LIMIT = 56 * 1024 * 1024

TM_MIX = 512
TQ_TOPK = 256
TG_BUILD = 64
GATE_UNROLL = 64
TB_DENSE = 1024
EB_DENSE = 1024
EC_DENSE = 512
KEY1_PER_BLOCK = EB_DENSE // PEER_KEYS
F32_SUBLANES = 8
TM_FINAL = 512
TN_MOD = 1024

_NEG_INF = float("-inf")


def _cparams(sem):
    return pltpu.CompilerParams(dimension_semantics=sem, vmem_limit_bytes=VMEM_LIMIT)


def _rms(x, g):
    return x * lax.rsqrt(jnp.mean(x * x, axis=-1, keepdims=True) + EPS) * g


def _mod_kernel(c_ref, w_ref, b_ref, o_ref):
    c = c_ref[...]
    cond = c / (1.0 + jnp.exp(-c))
    w = w_ref[0]
    o_ref[0] = jnp.sum(w * cond, axis=0, keepdims=True) + b_ref[0]


def _mod_call(c, w_mod, b_mod):
    n = N_MOD * D_MODEL
    return pl.pallas_call(
        _mod_kernel,
        grid=(DEPTH, n // TN_MOD),
        in_specs=[
            pl.BlockSpec((D_MODEL, 1), lambda l, j: (0, 0)),
            pl.BlockSpec((1, D_MODEL, TN_MOD), lambda l, j: (l, 0, j)),
            pl.BlockSpec((1, 1, TN_MOD), lambda l, j: (l, 0, j)),
        ],
        out_specs=pl.BlockSpec((1, 1, TN_MOD), lambda l, j: (l, 0, j)),
        out_shape=jax.ShapeDtypeStruct((DEPTH, 1, n), F32),
        compiler_params=_cparams(("arbitrary", "arbitrary")),
        name="mod_gemv",
    )(c.reshape(D_MODEL, 1), w_mod, b_mod.reshape(DEPTH, 1, n))


def _mix_in_body(x, n1g, sc1, sh1, win_ref, bin_ref, lng, lnb, wm_ref, bs_ref, bga,
                 ma_ref, xb_ref):
    h = (_rms(x, n1g) * (1.0 + sc1) + sh1).astype(BF16)
    proj = jnp.dot(h, win_ref[...], preferred_element_type=F32) + bin_ref[...]
    z = jax.nn.gelu(proj[:, :2 * WIDTH_A])
    u = z[:, :WIDTH_A]
    v = z[:, WIDTH_A:]
    mu = jnp.mean(v, axis=-1, keepdims=True)
    vc = v - mu
    var = jnp.mean(vc * vc, axis=-1, keepdims=True)
    vn = (vc * lax.rsqrt(var + EPS) * lng + lnb).astype(BF16)
    tm = x.shape[0]
    rows = []
    for n in range(tm // GMLP_BLOCK):
        vb = vn[n * GMLP_BLOCK:(n + 1) * GMLP_BLOCK, :]
        cols = []
        for hh in range(GMLP_HEADS):
            cols.append(jnp.dot(wm_ref[hh], vb[:, hh * LANES:(hh + 1) * LANES],
                                preferred_element_type=F32))
        rows.append(jnp.concatenate(cols, axis=1) + bs_ref[...])
    s = jnp.concatenate(rows, axis=0)
    out_a = u * s
    ma_ref[...] = _rms(out_a, bga).astype(BF16)
    xb_ref[...] = proj[:, 2 * WIDTH_A:]


def _mix_in_kernel0(x_ref, n1g, sc1, sh1, win_ref, bin_ref, lng, lnb, wm_ref, bs_ref, bga,
                    ma_ref, xb_ref):
    _mix_in_body(x_ref[...], n1g[...], sc1[...], sh1[...], win_ref, bin_ref, lng[...], lnb[...],
                 wm_ref, bs_ref, bga[...], ma_ref, xb_ref)


def _mix_in_kernel1(x_ref, p_ref, g2_ref, n1g, sc1, sh1, win_ref, bin_ref, lng, lnb, wm_ref,
                    bs_ref, bga, xo_ref, ma_ref, xb_ref):
    x = x_ref[...] + g2_ref[...] * p_ref[...]
    xo_ref[...] = x
    _mix_in_body(x, n1g[...], sc1[...], sh1[...], win_ref, bin_ref, lng[...], lnb[...],
                 wm_ref, bs_ref, bga[...], ma_ref, xb_ref)


def _row_spec(width):
    return pl.BlockSpec((1, width), lambda i: (0, 0))


def _tile_spec(tm, width):
    return pl.BlockSpec((tm, width), lambda i: (i, 0))


def _full_spec(shape):
    nd = len(shape)
    return pl.BlockSpec(shape, lambda i: (0,) * nd)


def _layer_spec(shape, layer):
    nd = len(shape)
    return pl.BlockSpec((None,) + tuple(shape), lambda i: (layer,) + (0,) * nd,
                        pipeline_mode=pl.Buffered(1))


def _mix_in_call(x, res, n1g, sc1, sh1, w_in, layer, b_in, lng, lnb, wm, bs_full, bga):
    tm = TM_MIX
    common_specs = [
        _row_spec(D_MODEL), _row_spec(D_MODEL), _row_spec(D_MODEL),
        _layer_spec((D_MODEL, IN_WIDTH), layer), _row_spec(IN_WIDTH),
        _row_spec(WIDTH_A), _row_spec(WIDTH_A),
        _full_spec((GMLP_HEADS, GMLP_BLOCK, GMLP_BLOCK)),
        _full_spec((GMLP_BLOCK, WIDTH_A)),
        _row_spec(WIDTH_A),
    ]
    common_args = (n1g, sc1, sh1, w_in, b_in, lng, lnb, wm, bs_full, bga)
    out_specs = [_tile_spec(tm, WIDTH_A), _tile_spec(tm, WIDTH_B)]
    out_shape = [jax.ShapeDtypeStruct((SEQ, WIDTH_A), BF16),
                 jax.ShapeDtypeStruct((SEQ, WIDTH_B), F32)]
    if res is None:
        ma, xb = pl.pallas_call(
            _mix_in_kernel0, grid=(SEQ // tm,),
            in_specs=[_tile_spec(tm, D_MODEL)] + common_specs,
            out_specs=out_specs, out_shape=out_shape,
            compiler_params=_cparams(("arbitrary",)), name="mix_in",
        )(x, *common_args)
        return x, ma, xb
    peer, g2 = res
    xo, ma, xb = pl.pallas_call(
        _mix_in_kernel1, grid=(SEQ // tm,),
        in_specs=[_tile_spec(tm, D_MODEL), _tile_spec(tm, D_MODEL), _row_spec(D_MODEL)]
        + common_specs,
        out_specs=[_tile_spec(tm, D_MODEL)] + out_specs,
        out_shape=[jax.ShapeDtypeStruct((SEQ, D_MODEL), F32)] + out_shape,
        compiler_params=_cparams(("arbitrary",)), name="mix_in_res",
    )(x, peer, g2, *common_args)
    return xo, ma, xb


def _s5_local_kernel(x_ref, w_ref, o_ref):
    o_ref[...] = jnp.dot(x_ref[0], w_ref[0], preferred_element_type=F32)


def _s5_scan_kernel(l_ref, ca_ref, cb_ref, h_ref):
    ca = ca_ref[...]
    cb = cb_ref[...]
    n_half = ca.shape[1] // LANES

    def step(k, s):
        h_ref[k] = s
        pieces = []
        for j in range(0, n_half, 2):
            pieces.append(s[:, (j + 1) * LANES:(j + 2) * LANES])
            pieces.append(s[:, j * LANES:(j + 1) * LANES])
        sw = jnp.concatenate(pieces, axis=1)
        return ca * s + cb * sw + l_ref[k]

    lax.fori_loop(0, N_CHUNKS, step, jnp.zeros(ca.shape, F32))


def _s5_out_kernel(x_ref, m_ref, h_ref, wo_ref, y_ref):
    y = jnp.dot(x_ref[0], m_ref[0], preferred_element_type=F32)
    h = h_ref[...]
    h_hi = h.astype(BF16)
    h_lo = (h - h_hi.astype(F32)).astype(BF16)
    y += jnp.dot(h_hi, wo_ref[0], preferred_element_type=F32)
    y += jnp.dot(h_lo, wo_ref[0], preferred_element_type=F32)
    y_ref[0] = y.astype(y_ref.dtype)


def _s5_call(xb, prep, layer):
    m_toep, w_state, w_out, ca, cb = prep
    g = SSM_GROUPS
    xg = xb.reshape(N_CHUNKS, CHUNK, g, SSM_GROUP).transpose(2, 0, 1, 3)
    xg = xg.reshape(g, N_CHUNKS, CHUNK_W).astype(BF16)
    group_spec = lambda *shape: pl.BlockSpec((None, 1) + shape, lambda i: (layer, i, 0, 0))
    local = pl.pallas_call(
        _s5_local_kernel, grid=(g,),
        in_specs=[pl.BlockSpec((1, N_CHUNKS, CHUNK_W), lambda i: (i, 0, 0)),
                  group_spec(CHUNK_W, 4 * SSM_STATE)],
        out_specs=pl.BlockSpec((N_CHUNKS, 4 * SSM_STATE), lambda i: (0, i)),
        out_shape=jax.ShapeDtypeStruct((N_CHUNKS, g * 4 * SSM_STATE), F32),
        compiler_params=_cparams(("arbitrary",)), name="s5_local",
    )(xg, w_state)
    rows = 8
    width = g * 4 * SSM_STATE // rows
    hstart = pl.pallas_call(
        _s5_scan_kernel,
        out_shape=jax.ShapeDtypeStruct((N_CHUNKS, rows, width), F32),
        compiler_params=pltpu.CompilerParams(vmem_limit_bytes=VMEM_LIMIT), name="s5_scan",
    )(local.reshape(N_CHUNKS, rows, width), ca[layer], cb[layer])
    hstart = hstart.reshape(N_CHUNKS, g * 4 * SSM_STATE)
    yg = pl.pallas_call(
        _s5_out_kernel, grid=(g,),
        in_specs=[pl.BlockSpec((1, N_CHUNKS, CHUNK_W), lambda i: (i, 0, 0)),
                  group_spec(CHUNK_W, CHUNK_W),
                  pl.BlockSpec((N_CHUNKS, 2 * SSM_STATE), lambda i: (0, 2 * i)),
                  group_spec(2 * SSM_STATE, CHUNK_W)],
        out_specs=pl.BlockSpec((1, N_CHUNKS, CHUNK_W), lambda i: (i, 0, 0)),
        out_shape=jax.ShapeDtypeStruct((g, N_CHUNKS, CHUNK_W), BF16),
        compiler_params=_cparams(("arbitrary",)), name="s5_out",
    )(xg, m_toep, hstart, w_out)
    y = yg.reshape(g, N_CHUNKS, CHUNK, SSM_GROUP).transpose(1, 2, 0, 3)
    return y.reshape(SEQ, WIDTH_B)


def _s5_prep(lam_re, lam_im, log_dt, b_re, b_im, c_re, c_im):
    hp = lax.Precision.HIGHEST
    cmul = lambda ar, ai, br, bi: (ar * br - ai * bi, ar * bi + ai * br)
    dt = jnp.exp(log_dt)[:, None]
    steps = jnp.arange(0, CHUNK + 1, dtype=F32)[:, None, None]
    mag = jnp.exp(lam_re[None] * (dt[None] * steps))
    ang = lam_im[None] * (dt[None] * steps)
    pw_r, pw_i = mag * jnp.cos(ang), mag * jnp.sin(ang)
    n_r, n_i = pw_r[1] - 1.0, pw_i[1]
    den = lam_re * lam_re + lam_im * lam_im
    q_r = (n_r * lam_re + n_i * lam_im) / den
    q_i = (n_i * lam_re - n_r * lam_im) / den
    bb_r, bb_i = cmul(q_r[..., None], q_i[..., None], b_re, b_im)
    e_r, e_i = cmul(pw_r[:CHUNK, :, :, None], pw_i[:CHUNK, :, :, None], bb_r[None], bb_i[None])
    k_tap = jnp.einsum("ghp,tgpk->tghk", jnp.concatenate([c_re, -c_im], axis=-1),
                       jnp.concatenate([e_r, e_i], axis=2), precision=hp)
    krow = k_tap.transpose(1, 3, 0, 2).reshape(SSM_GROUPS, SSM_GROUP, CHUNK_W).astype(BF16)
    kpad = jnp.concatenate([jnp.zeros_like(krow), krow], axis=-1)
    m_toep = jnp.stack([kpad[:, :, CHUNK_W - SSM_GROUP * c:2 * CHUNK_W - SSM_GROUP * c]
                        for c in range(CHUNK)], axis=1).reshape(SSM_GROUPS, CHUNK_W, CHUNK_W)
    rev_r, rev_i = pw_r[CHUNK - 1::-1][:CHUNK], pw_i[CHUNK - 1::-1][:CHUNK]
    w_r, w_i = cmul(rev_r[..., None], rev_i[..., None], bb_r[None], bb_i[None])
    lay = lambda w: w.transpose(1, 0, 3, 2).reshape(SSM_GROUPS, CHUNK_W, SSM_STATE)
    w_r, w_i = lay(w_r), lay(w_i)
    w_state = jnp.concatenate([w_r, w_i, w_i, w_r], axis=-1).astype(BF16)
    ca_r, ca_i = cmul(c_re[None], c_im[None], pw_r[1:CHUNK + 1, :, None, :],
                      pw_i[1:CHUNK + 1, :, None, :])
    lay = lambda w: w.transpose(1, 3, 0, 2).reshape(SSM_GROUPS, SSM_STATE, CHUNK_W)
    w_out = jnp.concatenate([lay(ca_r), -lay(ca_i)], axis=1).astype(BF16)
    ar, ai = pw_r[CHUNK], pw_i[CHUNK]
    ca = jnp.concatenate([ar, ar, ar, ar], axis=-1).reshape(8, -1)
    cb = jnp.concatenate([-ai, ai, ai, -ai], axis=-1).reshape(8, -1)
    return m_toep, w_state, w_out, ca, cb


def _mix_out_kernel(x_ref, ma_ref, xb_ref, ys_ref, d_ref, gw_ref, gb_ref, bgb_ref, wo_ref,
                    g1_ref, n2g_ref, sc2_ref, sh2_ref, xo_ref, h2_ref):
    y = ys_ref[...].astype(F32) + d_ref[...] * xb_ref[...]
    g = jax.nn.gelu(y)
    gate = jax.nn.sigmoid(jnp.dot(g.astype(BF16), gw_ref[...], preferred_element_type=F32)
                          + gb_ref[...])
    mb = _rms(g * gate, bgb_ref[...]).astype(BF16)
    mo = jnp.dot(ma_ref[...], wo_ref[:WIDTH_A, :], preferred_element_type=F32)
    mo += jnp.dot(mb, wo_ref[WIDTH_A:, :], preferred_element_type=F32)
    x = x_ref[...] + g1_ref[...] * mo
    xo_ref[...] = x
    h2_ref[...] = (_rms(x, n2g_ref[...]) * (1.0 + sc2_ref[...]) + sh2_ref[...]).astype(BF16)


def _mix_out_call(x, ma, xb, ys, d, glu_w, layer, glu_b, bgb, w_out, g1, n2g, sc2, sh2):
    tm = TM_MIX
    return pl.pallas_call(
        _mix_out_kernel, grid=(SEQ // tm,),
        in_specs=[_tile_spec(tm, D_MODEL), _tile_spec(tm, WIDTH_A), _tile_spec(tm, WIDTH_B),
                  _tile_spec(tm, WIDTH_B), _row_spec(WIDTH_B),
                  _layer_spec((WIDTH_B, WIDTH_B), layer), _row_spec(WIDTH_B), _row_spec(WIDTH_B),
                  _layer_spec((D_MODEL, D_MODEL), layer),
                  _row_spec(D_MODEL), _row_spec(D_MODEL), _row_spec(D_MODEL), _row_spec(D_MODEL)],
        out_specs=[_tile_spec(tm, D_MODEL), _tile_spec(tm, D_MODEL)],
        out_shape=[jax.ShapeDtypeStruct((SEQ, D_MODEL), F32),
                   jax.ShapeDtypeStruct((SEQ, D_MODEL), BF16)],
        compiler_params=_cparams(("arbitrary",)), name="mix_out",
    )(x, ma, xb, ys, d, glu_w, glu_b, bgb, w_out, g1, n2g, sc2, sh2)


_SORT_GROUP = 4
_ROW_BITS = 7


def _sorted_groups(x):
    n, w = x.shape
    assert n <= 1 << _ROW_BITS
    base = lax.broadcasted_iota(jnp.int32, (F32_SUBLANES, w), 0)
    groups = []
    for j in range(n // (F32_SUBLANES * _SORT_GROUP)):
        tile = lambda i: (j * _SORT_GROUP + i) * F32_SUBLANES
        v = [x[tile(i):tile(i) + F32_SUBLANES, :] for i in range(_SORT_GROUP)]
        r = [base + tile(i) for i in range(_SORT_GROUP)]

        def exchange(a, b, rows_ordered):
            if rows_ordered:
                first = v[a] >= v[b]
            else:
                first = jnp.logical_or(v[a] > v[b],
                                       jnp.logical_and(v[a] == v[b], r[a] < r[b]))
            v[a], v[b] = jnp.where(first, v[a], v[b]), jnp.where(first, v[b], v[a])
            r[a], r[b] = jnp.where(first, r[a], r[b]), jnp.where(first, r[b], r[a])

        exchange(0, 1, True)
        exchange(2, 3, True)
        exchange(0, 2, True)
        exchange(1, 3, True)
        exchange(1, 2, False)
        packed = r[0]
        for i in range(1, _SORT_GROUP):
            packed = packed | (r[i] << (i * _ROW_BITS))
        groups.append((tuple(v), packed))
    return tuple(groups)


def _rows_step(it, state, n, kiota):
    groups, vals, idxs = state
    last = _SORT_GROUP - 1
    heads = [v[0] for v, _ in groups]
    head_rows = [(rows & ((1 << _ROW_BITS) - 1)).astype(F32) for _, rows in groups]
    m = jnp.max(functools.reduce(jnp.maximum, heads), axis=0, keepdims=True)
    cands = [jnp.where(h == m, hr, float(n)) for h, hr in zip(heads, head_rows)]
    am = jnp.min(functools.reduce(jnp.minimum, cands), axis=0, keepdims=True)
    new_groups = []
    for (v, rows), hr in zip(groups, head_rows):
        won = hr == am
        nv = tuple(jnp.where(won, v[i + 1], v[i]) for i in range(last))
        nv += (jnp.where(won, _NEG_INF, v[last]),)
        new_groups.append((nv, jnp.where(won, rows >> _ROW_BITS, rows)))
    return (tuple(new_groups), jnp.where(kiota == it, m, vals), jnp.where(kiota == it, am, idxs))


def _pairs_step(it, state, lists, kiota, aiota):
    front, ptr, vals, ids = state
    v1, i1, v2, i2 = lists
    k = front.shape[0]
    m = jnp.max(front, axis=0, keepdims=True)
    a_sel = jnp.min(jnp.where(front == m, aiota, float(k)), axis=0, keepdims=True)
    won = aiota == a_sel
    b_sel = jnp.max(jnp.where(won, ptr, -1.0), axis=0, keepdims=True)
    e1 = jnp.max(jnp.where(won, i1, -1.0), axis=0, keepdims=True)
    e2 = jnp.max(jnp.where(aiota == b_sel, i2, -1.0), axis=0, keepdims=True)
    v2_next = jnp.max(jnp.where(aiota == b_sel + 1.0, v2, _NEG_INF), axis=0,
                      keepdims=True)
    return (jnp.where(won, v1 + v2_next, front), jnp.where(won, ptr + 1.0, ptr),
            jnp.where(kiota == it, m, vals),
            jnp.where(kiota == it, e1 * float(PEER_KEYS) + e2, ids))


def _select(xs, problems, k):
    n = xs[0].shape[0] if xs else 0
    w = (xs[0] if xs else problems[0][0]).shape[1]
    kiota = lax.broadcasted_iota(jnp.int32, (k, w), 0)
    aiota = kiota.astype(F32)

    def body(it, carry):
        rows_states, pair_states = carry
        return (tuple(_rows_step(it, s, n, kiota) for s in rows_states),
                tuple(_pairs_step(it, s, p, kiota, aiota) for s, p in zip(pair_states, problems)))

    zeros = jnp.zeros((k, w), F32)
    init = (tuple((_sorted_groups(x), zeros, zeros) for x in xs),
            tuple((v1 + v2[0:1, :], zeros, zeros, zeros) for v1, _, v2, _ in problems))
    rows_res, pair_res = lax.fori_loop(0, k, body, init)
    return ([(vals, idxs) for _, vals, idxs in rows_res],
            [(vals, ids) for _, _, vals, ids in pair_res])


def _peer_topk_kernel(h2_ref, wqt_ref, keys_ref, e_ref, g_ref, qt_ref):
    qt_ref[...] = lax.dot_general(wqt_ref[...], h2_ref[...], (((1,), (1,)), ((), ())),
                                  preferred_element_type=F32).astype(BF16)
    n_tiles = h2_ref.shape[0] // LANES
    tiles = [slice(t * LANES, (t + 1) * LANES) for t in range(n_tiles)]

    def emit(hh, pair_res):
        out0 = pl.multiple_of(hh * PEER_TOPK, PEER_TOPK)
        for lanes, (best, ids) in zip(tiles, pair_res):
            ex = jnp.exp(best - jnp.max(best, axis=0, keepdims=True))
            gate = ex / jnp.sum(ex, axis=0, keepdims=True)
            e_ref[pl.ds(out0, PEER_TOPK), lanes] = ids.astype(jnp.int32)
            g_ref[pl.ds(out0, PEER_TOPK), lanes] = gate

    def head(hh, _):
        sides = []
        for s in range(2):
            row0 = pl.multiple_of(hh * (2 * PEER_KEYS) + s * PEER_KEYS, PEER_KEYS)
            sides.append(jnp.dot(keys_ref[hh * 2 + s], qt_ref[pl.ds(row0, PEER_KEYS), :],
                                 preferred_element_type=F32))
        problems = []
        for lanes in tiles:
            (top1, top2), _ = _select([sides[0][:, lanes], sides[1][:, lanes]], [], PEER_TOPK)
            problems.append(top1 + top2)
        emit(hh, _select([], problems, PEER_TOPK)[1])
        return 0

    lax.fori_loop(0, PEER_HEADS, head, 0)


def _peer_topk_call(h2, wq_t, layer, keys):
    tq = TQ_TOPK
    n_slots = PEER_HEADS * PEER_TOPK
    return pl.pallas_call(
        _peer_topk_kernel, grid=(SEQ // tq,),
        in_specs=[_tile_spec(tq, D_MODEL), _layer_spec((D_MODEL, D_MODEL), layer),
                  _full_spec((2 * PEER_HEADS, PEER_KEYS, PEER_KEYS))],
        out_specs=[pl.BlockSpec((n_slots, tq), lambda i: (0, i)),
                   pl.BlockSpec((n_slots, tq), lambda i: (0, i))],
        out_shape=[jax.ShapeDtypeStruct((n_slots, SEQ), jnp.int32),
                   jax.ShapeDtypeStruct((n_slots, SEQ), F32)],
        scratch_shapes=[pltpu.VMEM((D_MODEL, tq), BF16)],
        compiler_params=_cparams(("arbitrary",)), name="peer_topk",
    )(h2, wq_t, keys)


def _peer_gate_kernel(e_ref, g_ref, o_ref):
    tg, n_slots = e_ref.shape
    kio = lax.broadcasted_iota(jnp.int32, (PEER_KEYS, n_slots), 0)

    def token(t, _):
        e = e_ref[pl.ds(t, 1), :]
        gate = g_ref[pl.ds(t, 1), :]
        i1 = e >> 7
        i2 = e & (PEER_KEYS - 1)
        oh1 = jnp.where(i1 == kio, 1.0, 0.0).astype(BF16)
        w2 = jnp.where(i2 == kio, gate, 0.0).astype(BF16)
        o_ref[t] = lax.dot_general(oh1, w2, (((1,), (1,)), ((), ())),
                                   preferred_element_type=F32)
        return 0

    lax.fori_loop(0, tg, token, 0, unroll=GATE_UNROLL)


def _peer_gate_call(e_rows, g_rows):
    tg = TG_BUILD
    n_slots = PEER_HEADS * PEER_TOPK
    return pl.pallas_call(
        _peer_gate_kernel, grid=(SEQ // tg,),
        in_specs=[_tile_spec(tg, n_slots), _tile_spec(tg, n_slots)],
        out_specs=pl.BlockSpec((tg, PEER_KEYS, PEER_KEYS), lambda i: (i, 0, 0)),
        out_shape=jax.ShapeDtypeStruct((SEQ, PEER_KEYS, PEER_KEYS), F32),
        compiler_params=_cparams(("arbitrary",)), name="peer_gate",
    )(e_rows, g_rows)


def _peer_dense_kernel(h2_ref, u_ref, v_ref, g_ref, o_ref):
    @pl.when(pl.program_id(1) == 0)
    def _():
        o_ref[...] = jnp.zeros_like(o_ref)

    h2 = h2_ref[...]
    tb = h2_ref.shape[0]
    g_rows = g_ref.reshape(tb * KEY1_PER_BLOCK, PEER_KEYS)
    keys_per_chunk = EC_DENSE // PEER_KEYS
    n_chunks = EB_DENSE // EC_DENSE
    chunk = lambda c: slice(c * EC_DENSE, (c + 1) * EC_DENSE)
    score = lambda c: lax.dot_general(h2, u_ref[chunk(c), :], (((1,), (1,)), ((), ())),
                                      preferred_element_type=F32)
    a_next = score(0)
    for c in range(n_chunks):
        a = a_next
        if c + 1 < n_chunks:
            a_next = score(c + 1)
        gate = jnp.concatenate(
            [g_rows[pl.ds(c * keys_per_chunk + k, tb, stride=KEY1_PER_BLOCK), :]
             for k in range(keys_per_chunk)], axis=1)
        p = (gate * jax.nn.gelu(a)).astype(BF16)
        o_ref[...] += jnp.dot(p, v_ref[chunk(c), :], preferred_element_type=F32)


def _peer_dense_call(h2, u_tab, v_tab, layer, gmat):
    tb, eb = TB_DENSE, EB_DENSE
    return pl.pallas_call(
        _peer_dense_kernel, grid=(SEQ // tb, PEER_EXPERTS // eb),
        in_specs=[pl.BlockSpec((tb, D_MODEL), lambda i, j: (i, 0)),
                  pl.BlockSpec((None, eb, D_MODEL), lambda i, j: (layer, j, 0)),
                  pl.BlockSpec((None, eb, D_MODEL), lambda i, j: (layer, j, 0)),
                  pl.BlockSpec((tb, KEY1_PER_BLOCK, PEER_KEYS), lambda i, j: (i, j, 0))],
        out_specs=pl.BlockSpec((tb, D_MODEL), lambda i, j: (i, 0)),
        out_shape=jax.ShapeDtypeStruct((SEQ, D_MODEL), F32),
        compiler_params=_cparams(("arbitrary", "arbitrary")), name="peer_dense",
    )(h2, u_tab, v_tab, gmat)


def _final_kernel(x_ref, p_ref, g2_ref, fg_ref, o_ref):
    x = x_ref[...] + g2_ref[...] * p_ref[...]
    o_ref[...] = _rms(x, fg_ref[...])


def _final_call(x, peer, g2, fg):
    tm = TM_FINAL
    return pl.pallas_call(
        _final_kernel, grid=(SEQ // tm,),
        in_specs=[_tile_spec(tm, D_MODEL), _tile_spec(tm, D_MODEL), _row_spec(D_MODEL),
                  _row_spec(D_MODEL)],
        out_specs=_tile_spec(tm, D_MODEL),
        out_shape=jax.ShapeDtypeStruct((SEQ, D_MODEL), F32),
        compiler_params=_cparams(("arbitrary",)), name="final_norm",
    )(x, peer, g2, fg)


def _chunk_causal_mask(n):
    blk = jnp.arange(n) // CHUNK
    return blk[:, None] >= blk[None, :]


def kernel(x, c, w_mod, b_mod, norm1_g, w_in, b_in, gm_ln_g, gm_ln_b, gm_ws, gm_bs, ssm_lam_re, ssm_lam_im, ssm_log_dt, ssm_b_re, ssm_b_im, ssm_c_re, ssm_c_im, ssm_d, glu_w, glu_b, branch_g, w_out, norm2_g, peer_wq, peer_keys, peer_u, peer_v, final_g):
    assert x.shape == (1, SEQ, D_MODEL)
    xt = x.reshape(SEQ, D_MODEL)
    mod = _mod_call(c, w_mod, b_mod).reshape(DEPTH, N_MOD, 1, D_MODEL)
    row = lambda v: v.reshape(1, -1)
    res = None
    w_in_b, glu_w_b, w_out_b = w_in.astype(BF16), glu_w.astype(BF16), w_out.astype(BF16)
    wq_t_b = peer_wq.astype(BF16).transpose(0, 2, 1)
    u_b, v_b = peer_u.astype(BF16), peer_v.astype(BF16)
    s5_prep = jax.vmap(_s5_prep)(ssm_lam_re, ssm_lam_im, ssm_log_dt, ssm_b_re, ssm_b_im,
                                 ssm_c_re, ssm_c_im)
    for i in range(DEPTH):
        sh1, sc1, g1, sh2, sc2, g2 = (mod[i, k] for k in range(N_MOD))
        wm = jnp.where(_chunk_causal_mask(GMLP_BLOCK)[None], gm_ws[i], 0.0).astype(BF16)
        bs_full = jnp.repeat(gm_bs[i].T, LANES, axis=1)
        xt, ma, xb = _mix_in_call(
            xt, res, row(norm1_g[i]), sc1, sh1, w_in_b, i, row(b_in[i]),
            row(gm_ln_g[i]), row(gm_ln_b[i]), wm, bs_full, row(branch_g[i, :WIDTH_A]))
        ys = _s5_call(xb, s5_prep, i)
        xt, h2 = _mix_out_call(
            xt, ma, xb, ys, row(ssm_d[i]), glu_w_b, i, row(glu_b[i]),
            row(branch_g[i, WIDTH_A:]), w_out_b, g1, row(norm2_g[i]), sc2, sh2)
        keys = peer_keys[i].reshape(2 * PEER_HEADS, PEER_KEYS, PEER_KEYS).astype(BF16)
        e_t, gate_t = _peer_topk_call(h2, wq_t_b, i, keys)
        gmat = _peer_gate_call(e_t.T, gate_t.T)
        peer = _peer_dense_call(h2, u_b, v_b, i, gmat)
        res = (peer, g2)
    out = _final_call(xt, res[0], res[1], row(final_g))
    return out.reshape(1, SEQ, D_MODEL)
```

```python
import functools

import jax
import jax.numpy as jnp
from jax import lax
from jax.experimental import pallas as pl
from jax.experimental.pallas import tpu as pltpu

F32 = jnp.float32
BF16 = jnp.bfloat16

D_MODEL = 2048
SEQ = 16384
DEPTH = 2
CHUNK = 64
WIDTH_A = 1024
WIDTH_B = 1024
IN_WIDTH = 3072
GMLP_BLOCK = 128
GMLP_HEADS = 8
SSM_GROUP = 16
SSM_GROUPS = 64
SSM_STATE = 64
PEER_HEADS = 8
PEER_KEYS = 128
PEER_EXPERTS = PEER_KEYS * PEER_KEYS
PEER_TOPK = 16
N_MOD = 6
EPS = 1e-6

LANES = 128
N_CHUNKS = SEQ // CHUNK
CHUNK_W = CHUNK * SSM_GROUP
VMEM_LIMIT = 56 * 1024 * 1024

TM_MIX = 512
TQ_TOPK = 512
TG_BUILD = 64
GATE_UNROLL = 64
TB_DENSE = 1024
EB_DENSE = 1024
EC_DENSE = 512
KEY1_PER_BLOCK = EB_DENSE // PEER_KEYS
F32_SUBLANES = 8
TM_FINAL = 512
TN_MOD = 1024

_NEG_INF = float("-inf")


def _cparams(sem):
    return pltpu.CompilerParams(dimension_semantics=sem, vmem_limit_bytes=VMEM_LIMIT)


def _rms(x, g):
    return x * lax.rsqrt(jnp.mean(x * x, axis=-1, keepdims=True) + EPS) * g


def _mod_kernel(c_ref, w_ref, b_ref, o_ref):
    c = c_ref[...]
    cond = c / (1.0 + jnp.exp(-c))
    w = w_ref[0]
    o_ref[0] = jnp.sum(w * cond, axis=0, keepdims=True) + b_ref[0]


def _mod_call(c, w_mod, b_mod):
    n = N_MOD * D_MODEL
    return pl.pallas_call(
        _mod_kernel,
        grid=(DEPTH, n // TN_MOD),
        in_specs=[
            pl.BlockSpec((D_MODEL, 1), lambda l, j: (0, 0)),
            pl.BlockSpec((1, D_MODEL, TN_MOD), lambda l, j: (l, 0, j)),
            pl.BlockSpec((1, 1, TN_MOD), lambda l, j: (l, 0, j)),
        ],
        out_specs=pl.BlockSpec((1, 1, TN_MOD), lambda l, j: (l, 0, j)),
        out_shape=jax.ShapeDtypeStruct((DEPTH, 1, n), F32),
        compiler_params=_cparams(("arbitrary", "arbitrary")),
        name="mod_gemv",
    )(c.reshape(D_MODEL, 1), w_mod, b_mod.reshape(DEPTH, 1, n))


def _mix_in_body(x, n1g, sc1, sh1, win_ref, bin_ref, lng, lnb, wm_ref, bs_ref, bga,
                 ma_ref, xb_ref, xb16_ref):
    h = (_rms(x, n1g) * (1.0 + sc1) + sh1).astype(BF16)
    proj = jnp.dot(h, win_ref[...], preferred_element_type=F32) + bin_ref[...]
    z = jax.nn.gelu(proj[:, :2 * WIDTH_A])
    u = z[:, :WIDTH_A]
    v = z[:, WIDTH_A:]
    mu = jnp.mean(v, axis=-1, keepdims=True)
    vc = v - mu
    var = jnp.mean(vc * vc, axis=-1, keepdims=True)
    vn = (vc * lax.rsqrt(var + EPS) * lng + lnb).astype(BF16)
    tm = x.shape[0]
    rows = []
    for n in range(tm // GMLP_BLOCK):
        vb = vn[n * GMLP_BLOCK:(n + 1) * GMLP_BLOCK, :]
        cols = []
        for hh in range(GMLP_HEADS):
            cols.append(jnp.dot(wm_ref[hh], vb[:, hh * LANES:(hh + 1) * LANES],
                                preferred_element_type=F32))
        rows.append(jnp.concatenate(cols, axis=1) + bs_ref[...])
    s = jnp.concatenate(rows, axis=0)
    out_a = u * s
    ma_ref[...] = _rms(out_a, bga).astype(BF16)
    xb = proj[:, 2 * WIDTH_A:]
    xb_ref[...] = xb
    xb16_ref[...] = xb.astype(BF16)


def _mix_in_kernel0(x_ref, n1g, sc1, sh1, win_ref, bin_ref, lng, lnb, wm_ref, bs_ref, bga,
                    ma_ref, xb_ref, xb16_ref):
    _mix_in_body(x_ref[...], n1g[...], sc1[...], sh1[...], win_ref, bin_ref, lng[...], lnb[...],
                 wm_ref, bs_ref, bga[...], ma_ref, xb_ref, xb16_ref)


def _mix_in_kernel1(x_ref, p_ref, g2_ref, n1g, sc1, sh1, win_ref, bin_ref, lng, lnb, wm_ref,
                    bs_ref, bga, xo_ref, ma_ref, xb_ref, xb16_ref):
    x = x_ref[...] + g2_ref[...] * p_ref[...]
    xo_ref[...] = x
    _mix_in_body(x, n1g[...], sc1[...], sh1[...], win_ref, bin_ref, lng[...], lnb[...],
                 wm_ref, bs_ref, bga[...], ma_ref, xb_ref, xb16_ref)


def _row_spec(width):
    return pl.BlockSpec((1, width), lambda i: (0, 0))


def _tile_spec(tm, width):
    return pl.BlockSpec((tm, width), lambda i: (i, 0))


def _full_spec(shape):
    nd = len(shape)
    return pl.BlockSpec(shape, lambda i: (0,) * nd)


def _layer_spec(shape, layer):
    nd = len(shape)
    return pl.BlockSpec((None,) + tuple(shape), lambda i: (layer,) + (0,) * nd,
                        pipeline_mode=pl.Buffered(1))


def _mix_in_call(x, res, n1g, sc1, sh1, w_in, layer, b_in, lng, lnb, wm, bs_full, bga):
    tm = TM_MIX
    common_specs = [
        _row_spec(D_MODEL), _row_spec(D_MODEL), _row_spec(D_MODEL),
        _layer_spec((D_MODEL, IN_WIDTH), layer), _row_spec(IN_WIDTH),
        _row_spec(WIDTH_A), _row_spec(WIDTH_A),
        _full_spec((GMLP_HEADS, GMLP_BLOCK, GMLP_BLOCK)),
        _full_spec((GMLP_BLOCK, WIDTH_A)),
        _row_spec(WIDTH_A),
    ]
    common_args = (n1g, sc1, sh1, w_in, b_in, lng, lnb, wm, bs_full, bga)
    out_specs = [_tile_spec(tm, WIDTH_A), _tile_spec(tm, WIDTH_B), _tile_spec(tm, WIDTH_B)]
    out_shape = [jax.ShapeDtypeStruct((SEQ, WIDTH_A), BF16),
                 jax.ShapeDtypeStruct((SEQ, WIDTH_B), F32),
                 jax.ShapeDtypeStruct((SEQ, WIDTH_B), BF16)]
    if res is None:
        ma, xb, xb16 = pl.pallas_call(
            _mix_in_kernel0, grid=(SEQ // tm,),
            in_specs=[_tile_spec(tm, D_MODEL)] + common_specs,
            out_specs=out_specs, out_shape=out_shape,
            compiler_params=_cparams(("arbitrary",)), name="mix_in",
        )(x, *common_args)
        return x, ma, xb, xb16
    peer, g2 = res
    xo, ma, xb, xb16 = pl.pallas_call(
        _mix_in_kernel1, grid=(SEQ // tm,),
        in_specs=[_tile_spec(tm, D_MODEL), _tile_spec(tm, D_MODEL), _row_spec(D_MODEL)]
        + common_specs,
        out_specs=[_tile_spec(tm, D_MODEL)] + out_specs,
        out_shape=[jax.ShapeDtypeStruct((SEQ, D_MODEL), F32)] + out_shape,
        compiler_params=_cparams(("arbitrary",)), name="mix_in_res",
    )(x, peer, g2, *common_args)
    return xo, ma, xb, xb16


def _s5_local_kernel(x_ref, w_ref, o_ref):
    o_ref[...] = jnp.dot(x_ref[0], w_ref[0], preferred_element_type=F32)


def _s5_scan_kernel(l_ref, ca_ref, cb_ref, h_ref):
    ca = ca_ref[...]
    cb = cb_ref[...]
    n_half = ca.shape[1] // LANES

    def step(k, s):
        h_ref[k] = s
        pieces = []
        for j in range(0, n_half, 2):
            pieces.append(s[:, (j + 1) * LANES:(j + 2) * LANES])
            pieces.append(s[:, j * LANES:(j + 1) * LANES])
        sw = jnp.concatenate(pieces, axis=1)
        return ca * s + cb * sw + l_ref[k]

    lax.fori_loop(0, N_CHUNKS, step, jnp.zeros(ca.shape, F32))


def _s5_out_kernel(x_ref, m_ref, h_ref, wo_ref, y_ref):
    y = jnp.dot(x_ref[0], m_ref[0], preferred_element_type=F32)
    h = h_ref[...]
    h_hi = h.astype(BF16)
    h_lo = (h - h_hi.astype(F32)).astype(BF16)
    y += jnp.dot(h_hi, wo_ref[0], preferred_element_type=F32)
    y += jnp.dot(h_lo, wo_ref[0], preferred_element_type=F32)
    y_ref[0] = y.astype(y_ref.dtype)


def _s5_call(xb16, prep):
    m_toep, w_state, w_out, ca, cb = prep
    g = SSM_GROUPS
    xg = xb16.reshape(N_CHUNKS, CHUNK, g, SSM_GROUP).transpose(2, 0, 1, 3)
    xg = xg.reshape(g, N_CHUNKS, CHUNK_W)
    group_spec = lambda *shape: pl.BlockSpec((1,) + shape, lambda i: (i, 0, 0))
    local = pl.pallas_call(
        _s5_local_kernel, grid=(g,),
        in_specs=[pl.BlockSpec((1, N_CHUNKS, CHUNK_W), lambda i: (i, 0, 0)),
                  group_spec(CHUNK_W, 4 * SSM_STATE)],
        out_specs=pl.BlockSpec((N_CHUNKS, 4 * SSM_STATE), lambda i: (0, i)),
        out_shape=jax.ShapeDtypeStruct((N_CHUNKS, g * 4 * SSM_STATE), F32),
        compiler_params=_cparams(("arbitrary",)), name="s5_local",
    )(xg, w_state)
    rows = 8
    width = g * 4 * SSM_STATE // rows
    hstart = pl.pallas_call(
        _s5_scan_kernel,
        out_shape=jax.ShapeDtypeStruct((N_CHUNKS, rows, width), F32),
        compiler_params=pltpu.CompilerParams(vmem_limit_bytes=VMEM_LIMIT), name="s5_scan",
    )(local.reshape(N_CHUNKS, rows, width), ca, cb)
    hstart = hstart.reshape(N_CHUNKS, g * 4 * SSM_STATE)
    yg = pl.pallas_call(
        _s5_out_kernel, grid=(g,),
        in_specs=[pl.BlockSpec((1, N_CHUNKS, CHUNK_W), lambda i: (i, 0, 0)),
                  group_spec(CHUNK_W, CHUNK_W),
                  pl.BlockSpec((N_CHUNKS, 2 * SSM_STATE), lambda i: (0, 2 * i)),
                  group_spec(2 * SSM_STATE, CHUNK_W)],
        out_specs=pl.BlockSpec((1, N_CHUNKS, CHUNK_W), lambda i: (i, 0, 0)),
        out_shape=jax.ShapeDtypeStruct((g, N_CHUNKS, CHUNK_W), BF16),
        compiler_params=_cparams(("arbitrary",)), name="s5_out",
    )(xg, m_toep, hstart, w_out)
    y = yg.reshape(g, N_CHUNKS, CHUNK, SSM_GROUP).transpose(1, 2, 0, 3)
    return y.reshape(SEQ, WIDTH_B)


def _s5_prep(lam_re, lam_im, log_dt, b_re, b_im, c_re, c_im):
    hp = lax.Precision.HIGHEST
    cmul = lambda ar, ai, br, bi: (ar * br - ai * bi, ar * bi + ai * br)
    dt = jnp.exp(log_dt)[:, None]
    steps = jnp.arange(0, CHUNK + 1, dtype=F32)[:, None, None]
    mag = jnp.exp(lam_re[None] * (dt[None] * steps))
    ang = lam_im[None] * (dt[None] * steps)
    pw_r, pw_i = mag * jnp.cos(ang), mag * jnp.sin(ang)
    n_r, n_i = pw_r[1] - 1.0, pw_i[1]
    den = lam_re * lam_re + lam_im * lam_im
    q_r = (n_r * lam_re + n_i * lam_im) / den
    q_i = (n_i * lam_re - n_r * lam_im) / den
    bb_r, bb_i = cmul(q_r[..., None], q_i[..., None], b_re, b_im)
    e_r, e_i = cmul(pw_r[:CHUNK, :, :, None], pw_i[:CHUNK, :, :, None], bb_r[None], bb_i[None])
    k_tap = jnp.einsum("ghp,tgpk->tghk", jnp.concatenate([c_re, -c_im], axis=-1),
                       jnp.concatenate([e_r, e_i], axis=2), precision=hp)
    krow = k_tap.transpose(1, 3, 0, 2).reshape(SSM_GROUPS, SSM_GROUP, CHUNK_W).astype(BF16)
    kpad = jnp.concatenate([jnp.zeros_like(krow), krow], axis=-1)
    m_toep = jnp.stack([kpad[:, :, CHUNK_W - SSM_GROUP * c:2 * CHUNK_W - SSM_GROUP * c]
                        for c in range(CHUNK)], axis=1).reshape(SSM_GROUPS, CHUNK_W, CHUNK_W)
    rev_r, rev_i = pw_r[CHUNK - 1::-1][:CHUNK], pw_i[CHUNK - 1::-1][:CHUNK]
    w_r, w_i = cmul(rev_r[..., None], rev_i[..., None], bb_r[None], bb_i[None])
    lay = lambda w: w.transpose(1, 0, 3, 2).reshape(SSM_GROUPS, CHUNK_W, SSM_STATE)
    w_r, w_i = lay(w_r), lay(w_i)
    w_state = jnp.concatenate([w_r, w_i, w_i, w_r], axis=-1).astype(BF16)
    ca_r, ca_i = cmul(c_re[None], c_im[None], pw_r[1:CHUNK + 1, :, None, :],
                      pw_i[1:CHUNK + 1, :, None, :])
    lay = lambda w: w.transpose(1, 3, 0, 2).reshape(SSM_GROUPS, SSM_STATE, CHUNK_W)
    w_out = jnp.concatenate([lay(ca_r), -lay(ca_i)], axis=1).astype(BF16)
    ar, ai = pw_r[CHUNK], pw_i[CHUNK]
    ca = jnp.concatenate([ar, ar, ar, ar], axis=-1).reshape(8, -1)
    cb = jnp.concatenate([-ai, ai, ai, -ai], axis=-1).reshape(8, -1)
    return m_toep, w_state, w_out, ca, cb


def _mix_out_kernel(x_ref, ma_ref, xb_ref, ys_ref, d_ref, gw_ref, gb_ref, bgb_ref, wo_ref,
                    g1_ref, n2g_ref, sc2_ref, sh2_ref, xo_ref, h2_ref):
    y = ys_ref[...].astype(F32) + d_ref[...] * xb_ref[...]
    g = jax.nn.gelu(y)
    gate = jax.nn.sigmoid(jnp.dot(g.astype(BF16), gw_ref[...], preferred_element_type=F32)
                          + gb_ref[...])
    mb = _rms(g * gate, bgb_ref[...]).astype(BF16)
    mo = jnp.dot(ma_ref[...], wo_ref[:WIDTH_A, :], preferred_element_type=F32)
    mo += jnp.dot(mb, wo_ref[WIDTH_A:, :], preferred_element_type=F32)
    x = x_ref[...] + g1_ref[...] * mo
    xo_ref[...] = x
    h2_ref[...] = (_rms(x, n2g_ref[...]) * (1.0 + sc2_ref[...]) + sh2_ref[...]).astype(BF16)


def _mix_out_call(x, ma, xb, ys, d, glu_w, layer, glu_b, bgb, w_out, g1, n2g, sc2, sh2):
    tm = TM_MIX
    return pl.pallas_call(
        _mix_out_kernel, grid=(SEQ // tm,),
        in_specs=[_tile_spec(tm, D_MODEL), _tile_spec(tm, WIDTH_A), _tile_spec(tm, WIDTH_B),
                  _tile_spec(tm, WIDTH_B), _row_spec(WIDTH_B),
                  _layer_spec((WIDTH_B, WIDTH_B), layer), _row_spec(WIDTH_B), _row_spec(WIDTH_B),
                  _layer_spec((D_MODEL, D_MODEL), layer),
                  _row_spec(D_MODEL), _row_spec(D_MODEL), _row_spec(D_MODEL), _row_spec(D_MODEL)],
        out_specs=[_tile_spec(tm, D_MODEL), _tile_spec(tm, D_MODEL)],
        out_shape=[jax.ShapeDtypeStruct((SEQ, D_MODEL), F32),
                   jax.ShapeDtypeStruct((SEQ, D_MODEL), BF16)],
        compiler_params=_cparams(("arbitrary",)), name="mix_out",
    )(x, ma, xb, ys, d, glu_w, glu_b, bgb, w_out, g1, n2g, sc2, sh2)


_SORT_GROUP = 4
_ROW_BITS = 7


def _sorted_groups(x):
    n, w = x.shape
    assert n <= 1 << _ROW_BITS
    base = lax.broadcasted_iota(jnp.int32, (F32_SUBLANES, w), 0)
    groups = []
    for j in range(n // (F32_SUBLANES * _SORT_GROUP)):
        tile = lambda i: (j * _SORT_GROUP + i) * F32_SUBLANES
        v = [x[tile(i):tile(i) + F32_SUBLANES, :] for i in range(_SORT_GROUP)]
        r = [base + tile(i) for i in range(_SORT_GROUP)]

        def exchange(a, b, rows_ordered):
            if rows_ordered:
                first = v[a] >= v[b]
            else:
                first = jnp.logical_or(v[a] > v[b],
                                       jnp.logical_and(v[a] == v[b], r[a] < r[b]))
            v[a], v[b] = jnp.where(first, v[a], v[b]), jnp.where(first, v[b], v[a])
            r[a], r[b] = jnp.where(first, r[a], r[b]), jnp.where(first, r[b], r[a])

        exchange(0, 1, True)
        exchange(2, 3, True)
        exchange(0, 2, True)
        exchange(1, 3, True)
        exchange(1, 2, False)
        packed = r[0]
        for i in range(1, _SORT_GROUP):
            packed = packed | (r[i] << (i * _ROW_BITS))
        groups.append((tuple(v), packed))
    return tuple(groups)


def _rows_step(it, state, n, kiota):
    groups, vals, idxs = state
    last = _SORT_GROUP - 1
    heads = [v[0] for v, _ in groups]
    head_rows = [(rows & ((1 << _ROW_BITS) - 1)).astype(F32) for _, rows in groups]
    m = jnp.max(functools.reduce(jnp.maximum, heads), axis=0, keepdims=True)
    cands = [jnp.where(h == m, hr, float(n)) for h, hr in zip(heads, head_rows)]
    am = jnp.min(functools.reduce(jnp.minimum, cands), axis=0, keepdims=True)
    new_groups = []
    for (v, rows), hr in zip(groups, head_rows):
        won = hr == am
        nv = tuple(jnp.where(won, v[i + 1], v[i]) for i in range(last))
        nv += (jnp.where(won, _NEG_INF, v[last]),)
        new_groups.append((nv, jnp.where(won, rows >> _ROW_BITS, rows)))
    return (tuple(new_groups), jnp.where(kiota == it, m, vals), jnp.where(kiota == it, am, idxs))


def _pairs_step(it, state, lists, kiota, aiota):
    front, ptr, vals, ids = state
    v1, i1, v2, i2 = lists
    k = front.shape[0]
    m = jnp.max(front, axis=0, keepdims=True)
    a_sel = jnp.min(jnp.where(front == m, aiota, float(k)), axis=0, keepdims=True)
    won = aiota == a_sel
    b_sel = jnp.max(jnp.where(won, ptr, -1.0), axis=0, keepdims=True)
    e1 = jnp.max(jnp.where(won, i1, -1.0), axis=0, keepdims=True)
    e2 = jnp.max(jnp.where(aiota == b_sel, i2, -1.0), axis=0, keepdims=True)
    v2_next = jnp.max(jnp.where(aiota == b_sel + 1.0, v2, _NEG_INF), axis=0,
                      keepdims=True)
    return (jnp.where(won, v1 + v2_next, front), jnp.where(won, ptr + 1.0, ptr),
            jnp.where(kiota == it, m, vals),
            jnp.where(kiota == it, e1 * float(PEER_KEYS) + e2, ids))


def _select(xs, problems, k):
    n = xs[0].shape[0] if xs else 0
    w = (xs[0] if xs else problems[0][0]).shape[1]
    kiota = lax.broadcasted_iota(jnp.int32, (k, w), 0)
    aiota = kiota.astype(F32)

    def body(it, carry):
        rows_states, pair_states = carry
        return (tuple(_rows_step(it, s, n, kiota) for s in rows_states),
                tuple(_pairs_step(it, s, p, kiota, aiota) for s, p in zip(pair_states, problems)))

    zeros = jnp.zeros((k, w), F32)
    init = (tuple((_sorted_groups(x), zeros, zeros) for x in xs),
            tuple((v1 + v2[0:1, :], zeros, zeros, zeros) for v1, _, v2, _ in problems))
    rows_res, pair_res = lax.fori_loop(0, k, body, init)
    return ([(vals, idxs) for _, vals, idxs in rows_res],
            [(vals, ids) for _, _, vals, ids in pair_res])


def _peer_topk_kernel(h2_ref, wqt_ref, keys_ref, e_ref, g_ref, qt_ref, sc_ref):
    qt_ref[...] = lax.dot_general(wqt_ref[...], h2_ref[...], (((1,), (1,)), ((), ())),
                                  preferred_element_type=F32).astype(BF16)
    for hs in range(2 * PEER_HEADS):
        rows = slice(hs * PEER_KEYS, (hs + 1) * PEER_KEYS)
        sc_ref[rows, :] = jnp.dot(keys_ref[hs], qt_ref[rows, :], preferred_element_type=F32)
    n_tiles = h2_ref.shape[0] // LANES
    tiles = [slice(t * LANES, (t + 1) * LANES) for t in range(n_tiles)]

    def emit(hh, pair_res):
        out0 = pl.multiple_of(hh * PEER_TOPK, PEER_TOPK)
        for lanes, (best, ids) in zip(tiles, pair_res):
            ex = jnp.exp(best - jnp.max(best, axis=0, keepdims=True))
            gate = ex / jnp.sum(ex, axis=0, keepdims=True)
            e_ref[pl.ds(out0, PEER_TOPK), lanes] = ids.astype(jnp.int32)
            g_ref[pl.ds(out0, PEER_TOPK), lanes] = gate

    def head(hh, _):
        sides = []
        for s in range(2):
            row0 = pl.multiple_of(hh * (2 * PEER_KEYS) + s * PEER_KEYS, PEER_KEYS)
            sides.append(sc_ref[pl.ds(row0, PEER_KEYS), :])
        problems = []
        for lanes in tiles:
            (top1, top2), _ = _select([sides[0][:, lanes], sides[1][:, lanes]], [], PEER_TOPK)
            problems.append(top1 + top2)
        emit(hh, _select([], problems, PEER_TOPK)[1])
        return 0

    lax.fori_loop(0, PEER_HEADS, head, 0)


def _peer_topk_call(h2, wq_t, layer, keys):
    tq = TQ_TOPK
    n_slots = PEER_HEADS * PEER_TOPK
    return pl.pallas_call(
        _peer_topk_kernel, grid=(SEQ // tq,),
        in_specs=[_tile_spec(tq, D_MODEL), _layer_spec((D_MODEL, D_MODEL), layer),
                  _full_spec((2 * PEER_HEADS, PEER_KEYS, PEER_KEYS))],
        out_specs=[pl.BlockSpec((n_slots, tq), lambda i: (0, i)),
                   pl.BlockSpec((n_slots, tq), lambda i: (0, i))],
        out_shape=[jax.ShapeDtypeStruct((n_slots, SEQ), jnp.int32),
                   jax.ShapeDtypeStruct((n_slots, SEQ), F32)],
        scratch_shapes=[pltpu.VMEM((D_MODEL, tq), BF16), pltpu.VMEM((D_MODEL, tq), F32)],
        compiler_params=_cparams(("arbitrary",)), name="peer_topk",
    )(h2, wq_t, keys)


def _peer_gate_kernel(e_ref, g_ref, o_ref):
    tg, n_slots = e_ref.shape
    kio = lax.broadcasted_iota(jnp.int32, (PEER_KEYS, n_slots), 0)

    def token(t, _):
        e = e_ref[pl.ds(t, 1), :]
        gate = g_ref[pl.ds(t, 1), :]
        i1 = e >> 7
        i2 = e & (PEER_KEYS - 1)
        oh1 = jnp.where(i1 == kio, 1.0, 0.0).astype(BF16)
        w2 = jnp.where(i2 == kio, gate, 0.0).astype(BF16)
        o_ref[t] = lax.dot_general(oh1, w2, (((1,), (1,)), ((), ())),
                                   preferred_element_type=F32)
        return 0

    lax.fori_loop(0, tg, token, 0, unroll=GATE_UNROLL)


def _peer_gate_call(e_rows, g_rows):
    tg = TG_BUILD
    n_slots = PEER_HEADS * PEER_TOPK
    return pl.pallas_call(
        _peer_gate_kernel, grid=(SEQ // tg,),
        in_specs=[_tile_spec(tg, n_slots), _tile_spec(tg, n_slots)],
        out_specs=pl.BlockSpec((tg, PEER_KEYS, PEER_KEYS), lambda i: (i, 0, 0)),
        out_shape=jax.ShapeDtypeStruct((SEQ, PEER_KEYS, PEER_KEYS), F32),
        compiler_params=_cparams(("arbitrary",)), name="peer_gate",
    )(e_rows, g_rows)


def _peer_dense_kernel(h2_ref, u_ref, v_ref, g_ref, o_ref):
    @pl.when(pl.program_id(1) == 0)
    def _():
        o_ref[...] = jnp.zeros_like(o_ref)

    h2 = h2_ref[...]
    tb = h2_ref.shape[0]
    g_rows = g_ref.reshape(tb * KEY1_PER_BLOCK, PEER_KEYS)
    keys_per_chunk = EC_DENSE // PEER_KEYS
    n_chunks = EB_DENSE // EC_DENSE
    chunk = lambda c: slice(c * EC_DENSE, (c + 1) * EC_DENSE)
    score = lambda c: lax.dot_general(h2, u_ref[chunk(c), :], (((1,), (1,)), ((), ())),
                                      preferred_element_type=F32)
    a_next = score(0)
    for c in range(n_chunks):
        a = a_next
        if c + 1 < n_chunks:
            a_next = score(c + 1)
        gate = jnp.concatenate(
            [g_rows[pl.ds(c * keys_per_chunk + k, tb, stride=KEY1_PER_BLOCK), :]
             for k in range(keys_per_chunk)], axis=1)
        p = (gate * jax.nn.gelu(a)).astype(BF16)
        o_ref[...] += jnp.dot(p, v_ref[chunk(c), :], preferred_element_type=F32)


def _peer_dense_call(h2, u_tab, v_tab, layer, gmat):
    tb, eb = TB_DENSE, EB_DENSE
    return pl.pallas_call(
        _peer_dense_kernel, grid=(SEQ // tb, PEER_EXPERTS // eb),
        in_specs=[pl.BlockSpec((tb, D_MODEL), lambda i, j: (i, 0)),
                  pl.BlockSpec((None, eb, D_MODEL), lambda i, j: (layer, j, 0)),
                  pl.BlockSpec((None, eb, D_MODEL), lambda i, j: (layer, j, 0)),
                  pl.BlockSpec((tb, KEY1_PER_BLOCK, PEER_KEYS), lambda i, j: (i, j, 0))],
        out_specs=pl.BlockSpec((tb, D_MODEL), lambda i, j: (i, 0)),
        out_shape=jax.ShapeDtypeStruct((SEQ, D_MODEL), F32),
        compiler_params=_cparams(("arbitrary", "arbitrary")), name="peer_dense",
    )(h2, u_tab, v_tab, gmat)


def _final_kernel(x_ref, p_ref, g2_ref, fg_ref, o_ref):
    x = x_ref[...] + g2_ref[...] * p_ref[...]
    o_ref[...] = _rms(x, fg_ref[...])


def _final_call(x, peer, g2, fg):
    tm = TM_FINAL
    return pl.pallas_call(
        _final_kernel, grid=(SEQ // tm,),
        in_specs=[_tile_spec(tm, D_MODEL), _tile_spec(tm, D_MODEL), _row_spec(D_MODEL),
                  _row_spec(D_MODEL)],
        out_specs=_tile_spec(tm, D_MODEL),
        out_shape=jax.ShapeDtypeStruct((SEQ, D_MODEL), F32),
        compiler_params=_cparams(("arbitrary",)), name="final_norm",
    )(x, peer, g2, fg)


def _chunk_causal_mask(n):
    blk = jnp.arange(n) // CHUNK
    return blk[:, None] >= blk[None, :]


def kernel(x, c, w_mod, b_mod, norm1_g, w_in, b_in, gm_ln_g, gm_ln_b, gm_ws, gm_bs, ssm_lam_re, ssm_lam_im, ssm_log_dt, ssm_b_re, ssm_b_im, ssm_c_re, ssm_c_im, ssm_d, glu_w, glu_b, branch_g, w_out, norm2_g, peer_wq, peer_keys, peer_u, peer_v, final_g):
    assert x.shape == (1, SEQ, D_MODEL)
    xt = x.reshape(SEQ, D_MODEL)
    mod = _mod_call(c, w_mod, b_mod).reshape(DEPTH, N_MOD, 1, D_MODEL)
    row = lambda v: v.reshape(1, -1)
    res = None
    w_in_b, glu_w_b, w_out_b = w_in.astype(BF16), glu_w.astype(BF16), w_out.astype(BF16)
    wq_t_b = peer_wq.astype(BF16).transpose(0, 2, 1)
    u_b, v_b = peer_u.astype(BF16), peer_v.astype(BF16)
    for i in range(DEPTH):
        sh1, sc1, g1, sh2, sc2, g2 = (mod[i, k] for k in range(N_MOD))
        wm = jnp.where(_chunk_causal_mask(GMLP_BLOCK)[None], gm_ws[i], 0.0).astype(BF16)
        bs_full = jnp.repeat(gm_bs[i].T, LANES, axis=1)
        xt, ma, xb, xb16 = _mix_in_call(
            xt, res, row(norm1_g[i]), sc1, sh1, w_in_b, i, row(b_in[i]),
            row(gm_ln_g[i]), row(gm_ln_b[i]), wm, bs_full, row(branch_g[i, :WIDTH_A]))
        prep = _s5_prep(ssm_lam_re[i], ssm_lam_im[i], ssm_log_dt[i], ssm_b_re[i], ssm_b_im[i],
                        ssm_c_re[i], ssm_c_im[i])
        ys = _s5_call(xb16, prep)
        xt, h2 = _mix_out_call(
            xt, ma, xb, ys, row(ssm_d[i]), glu_w_b, i, row(glu_b[i]),
            row(branch_g[i, WIDTH_A:]), w_out_b, g1, row(norm2_g[i]), sc2, sh2)
        keys = peer_keys[i].reshape(2 * PEER_HEADS, PEER_KEYS, PEER_KEYS).astype(BF16)
        e_t, gate_t = _peer_topk_call(h2, wq_t_b, i, keys)
        gmat = _peer_gate_call(e_t.T, gate_t.T)
        peer = _peer_dense_call(h2, u_b, v_b, i, gmat)
        res = (peer, g2)
    out = _final_call(xt, res[0], res[1], row(final_g))
    return out.reshape(1, SEQ, D_MODEL)
```

```python
import functools

import jax
import jax.numpy as jnp
from jax import lax
from jax.experimental import pallas as pl
from jax.experimental.pallas import tpu as pltpu

F32 = jnp.float32
BF16 = jnp.bfloat16

D_MODEL = 2048
SEQ = 16384
DEPTH = 2
CHUNK = 64
WIDTH_A = 1024
WIDTH_B = 1024
IN_WIDTH = 3072
GMLP_BLOCK = 128
GMLP_HEADS = 8
SSM_GROUP = 16
SSM_GROUPS = 64
SSM_STATE = 64
PEER_HEADS = 8
PEER_KEYS = 128
PEER_KEY_BITS = PEER_KEYS.bit_length() - 1
assert 1 << PEER_KEY_BITS == PEER_KEYS
PEER_EXPERTS = PEER_KEYS * PEER_KEYS
PEER_TOPK = 16
N_MOD = 6
EPS = 1e-6

LANES = 128
N_CHUNKS = SEQ // CHUNK
CHUNK_W = CHUNK * SSM_GROUP
VMEM_LIMIT = 56 * 1024 * 1024

TM_MIX = 512
TQ_TOPK = 512
TG_BUILD = 64
GATE_UNROLL = 64
TB_DENSE = 1024
EB_DENSE = 1024
EC_DENSE = 512
KEY1_PER_BLOCK = EB_DENSE // PEER_KEYS
F32_SUBLANES = 8
TM_FINAL = 512
TN_MOD = 1024

_NEG_INF = float("-inf")


def _cparams(sem):
    return pltpu.CompilerParams(dimension_semantics=sem, vmem_limit_bytes=VMEM_LIMIT)


def _rms(x, g):
    return x * lax.rsqrt(jnp.mean(x * x, axis=-1, keepdims=True) + EPS) * g


def _mod_kernel(c_ref, w_ref, b_ref, o_ref):
    c = c_ref[...]
    cond = c / (1.0 + jnp.exp(-c))
    w = w_ref[0]
    o_ref[0] = jnp.sum(w * cond, axis=0, keepdims=True) + b_ref[0]


def _mod_call(c, w_mod, b_mod):
    n = N_MOD * D_MODEL
    return pl.pallas_call(
        _mod_kernel,
        grid=(DEPTH, n // TN_MOD),
        in_specs=[
            pl.BlockSpec((D_MODEL, 1), lambda l, j: (0, 0)),
            pl.BlockSpec((1, D_MODEL, TN_MOD), lambda l, j: (l, 0, j)),
            pl.BlockSpec((1, 1, TN_MOD), lambda l, j: (l, 0, j)),
        ],
        out_specs=pl.BlockSpec((1, 1, TN_MOD), lambda l, j: (l, 0, j)),
        out_shape=jax.ShapeDtypeStruct((DEPTH, 1, n), F32),
        compiler_params=_cparams(("arbitrary", "arbitrary")),
        name="mod_gemv",
    )(c.reshape(D_MODEL, 1), w_mod, b_mod.reshape(DEPTH, 1, n))


def _mix_in_body(x, n1g, sc1, sh1, win_ref, bin_ref, lng, lnb, wm_ref, bs_ref, bga,
                 ma_ref, xb_ref, xb16_ref):
    h = (_rms(x, n1g) * (1.0 + sc1) + sh1).astype(BF16)
    proj = jnp.dot(h, win_ref[...], preferred_element_type=F32) + bin_ref[...]
    z = jax.nn.gelu(proj[:, :2 * WIDTH_A])
    u = z[:, :WIDTH_A]
    v = z[:, WIDTH_A:]
    mu = jnp.mean(v, axis=-1, keepdims=True)
    vc = v - mu
    var = jnp.mean(vc * vc, axis=-1, keepdims=True)
    vn = (vc * lax.rsqrt(var + EPS) * lng + lnb).astype(BF16)
    tm = x.shape[0]
    rows = []
    for n in range(tm // GMLP_BLOCK):
        vb = vn[n * GMLP_BLOCK:(n + 1) * GMLP_BLOCK, :]
        cols = []
        for hh in range(GMLP_HEADS):
            cols.append(jnp.dot(wm_ref[hh], vb[:, hh * LANES:(hh + 1) * LANES],
                                preferred_element_type=F32))
        rows.append(jnp.concatenate(cols, axis=1) + bs_ref[...])
    s = jnp.concatenate(rows, axis=0)
    out_a = u * s
    ma_ref[...] = _rms(out_a, bga).astype(BF16)
    xb = proj[:, 2 * WIDTH_A:]
    xb_ref[...] = xb
    xb16_ref[...] = xb.astype(BF16)


def _mix_in_kernel0(x_ref, n1g, sc1, sh1, win_ref, bin_ref, lng, lnb, wm_ref, bs_ref, bga,
                    ma_ref, xb_ref, xb16_ref):
    _mix_in_body(x_ref[...], n1g[...], sc1[...], sh1[...], win_ref, bin_ref, lng[...], lnb[...],
                 wm_ref, bs_ref, bga[...], ma_ref, xb_ref, xb16_ref)


def _mix_in_kernel1(x_ref, p_ref, g2_ref, n1g, sc1, sh1, win_ref, bin_ref, lng, lnb, wm_ref,
                    bs_ref, bga, xo_ref, ma_ref, xb_ref, xb16_ref):
    x = x_ref[...] + g2_ref[...] * p_ref[...]
    xo_ref[...] = x
    _mix_in_body(x, n1g[...], sc1[...], sh1[...], win_ref, bin_ref, lng[...], lnb[...],
                 wm_ref, bs_ref, bga[...], ma_ref, xb_ref, xb16_ref)


def _row_spec(width):
    return pl.BlockSpec((1, width), lambda i: (0, 0))


def _tile_spec(tm, width):
    return pl.BlockSpec((tm, width), lambda i: (i, 0))


def _full_spec(shape):
    nd = len(shape)
    return pl.BlockSpec(shape, lambda i: (0,) * nd)


def _layer_spec(shape, layer):
    nd = len(shape)
    return pl.BlockSpec((None,) + tuple(shape), lambda i: (layer,) + (0,) * nd,
                        pipeline_mode=pl.Buffered(1))


def _mix_in_call(x, res, n1g, sc1, sh1, w_in, layer, b_in, lng, lnb, wm, bs_full, bga):
    tm = TM_MIX
    common_specs = [
        _row_spec(D_MODEL), _row_spec(D_MODEL), _row_spec(D_MODEL),
        _layer_spec((D_MODEL, IN_WIDTH), layer), _row_spec(IN_WIDTH),
        _row_spec(WIDTH_A), _row_spec(WIDTH_A),
        _full_spec((GMLP_HEADS, GMLP_BLOCK, GMLP_BLOCK)),
        _full_spec((GMLP_BLOCK, WIDTH_A)),
        _row_spec(WIDTH_A),
    ]
    common_args = (n1g, sc1, sh1, w_in, b_in, lng, lnb, wm, bs_full, bga)
    out_specs = [_tile_spec(tm, WIDTH_A), _tile_spec(tm, WIDTH_B), _tile_spec(tm, WIDTH_B)]
    out_shape = [jax.ShapeDtypeStruct((SEQ, WIDTH_A), BF16),
                 jax.ShapeDtypeStruct((SEQ, WIDTH_B), F32),
                 jax.ShapeDtypeStruct((SEQ, WIDTH_B), BF16)]
    if res is None:
        ma, xb, xb16 = pl.pallas_call(
            _mix_in_kernel0, grid=(SEQ // tm,),
            in_specs=[_tile_spec(tm, D_MODEL)] + common_specs,
            out_specs=out_specs, out_shape=out_shape,
            compiler_params=_cparams(("arbitrary",)), name="mix_in",
        )(x, *common_args)
        return x, ma, xb, xb16
    peer, g2 = res
    xo, ma, xb, xb16 = pl.pallas_call(
        _mix_in_kernel1, grid=(SEQ // tm,),
        in_specs=[_tile_spec(tm, D_MODEL), _tile_spec(tm, D_MODEL), _row_spec(D_MODEL)]
        + common_specs,
        out_specs=[_tile_spec(tm, D_MODEL)] + out_specs,
        out_shape=[jax.ShapeDtypeStruct((SEQ, D_MODEL), F32)] + out_shape,
        compiler_params=_cparams(("arbitrary",)), name="mix_in_res",
    )(x, peer, g2, *common_args)
    return xo, ma, xb, xb16


def _s5_local_kernel(x_ref, w_ref, o_ref):
    o_ref[...] = jnp.dot(x_ref[0], w_ref[0], preferred_element_type=F32)


def _s5_scan_kernel(l_ref, ca_ref, cb_ref, h_ref):
    ca = ca_ref[...]
    cb = cb_ref[...]
    n_half = ca.shape[1] // LANES

    def step(k, s):
        h_ref[k] = s
        pieces = []
        for j in range(0, n_half, 2):
            pieces.append(s[:, (j + 1) * LANES:(j + 2) * LANES])
            pieces.append(s[:, j * LANES:(j + 1) * LANES])
        sw = jnp.concatenate(pieces, axis=1)
        return ca * s + cb * sw + l_ref[k]

    lax.fori_loop(0, N_CHUNKS, step, jnp.zeros(ca.shape, F32))


def _s5_out_kernel(x_ref, m_ref, h_ref, wo_ref, y_ref):
    y = jnp.dot(x_ref[0], m_ref[0], preferred_element_type=F32)
    h = h_ref[...]
    h_hi = h.astype(BF16)
    h_lo = (h - h_hi.astype(F32)).astype(BF16)
    y += jnp.dot(h_hi, wo_ref[0], preferred_element_type=F32)
    y += jnp.dot(h_lo, wo_ref[0], preferred_element_type=F32)
    y_ref[0] = y.astype(y_ref.dtype)


def _s5_call(xb16, prep):
    m_toep, w_state, w_out, ca, cb = prep
    g = SSM_GROUPS
    xg = xb16.reshape(N_CHUNKS, CHUNK, g, SSM_GROUP).transpose(2, 0, 1, 3)
    xg = xg.reshape(g, N_CHUNKS, CHUNK_W)
    group_spec = lambda *shape: pl.BlockSpec((1,) + shape, lambda i: (i, 0, 0))
    local = pl.pallas_call(
        _s5_local_kernel, grid=(g,),
        in_specs=[pl.BlockSpec((1, N_CHUNKS, CHUNK_W), lambda i: (i, 0, 0)),
                  group_spec(CHUNK_W, 4 * SSM_STATE)],
        out_specs=pl.BlockSpec((N_CHUNKS, 4 * SSM_STATE), lambda i: (0, i)),
        out_shape=jax.ShapeDtypeStruct((N_CHUNKS, g * 4 * SSM_STATE), F32),
        compiler_params=_cparams(("arbitrary",)), name="s5_local",
    )(xg, w_state)
    rows = 8
    width = g * 4 * SSM_STATE // rows
    hstart = pl.pallas_call(
        _s5_scan_kernel,
        out_shape=jax.ShapeDtypeStruct((N_CHUNKS, rows, width), F32),
        compiler_params=pltpu.CompilerParams(vmem_limit_bytes=VMEM_LIMIT), name="s5_scan",
    )(local.reshape(N_CHUNKS, rows, width), ca, cb)
    hstart = hstart.reshape(N_CHUNKS, g * 4 * SSM_STATE)
    yg = pl.pallas_call(
        _s5_out_kernel, grid=(g,),
        in_specs=[pl.BlockSpec((1, N_CHUNKS, CHUNK_W), lambda i: (i, 0, 0)),
                  group_spec(CHUNK_W, CHUNK_W),
                  pl.BlockSpec((N_CHUNKS, 2 * SSM_STATE), lambda i: (0, 2 * i)),
                  group_spec(2 * SSM_STATE, CHUNK_W)],
        out_specs=pl.BlockSpec((1, N_CHUNKS, CHUNK_W), lambda i: (i, 0, 0)),
        out_shape=jax.ShapeDtypeStruct((g, N_CHUNKS, CHUNK_W), BF16),
        compiler_params=_cparams(("arbitrary",)), name="s5_out",
    )(xg, m_toep, hstart, w_out)
    y = yg.reshape(g, N_CHUNKS, CHUNK, SSM_GROUP).transpose(1, 2, 0, 3)
    return y.reshape(SEQ, WIDTH_B)


def _s5_prep(lam_re, lam_im, log_dt, b_re, b_im, c_re, c_im):
    hp = lax.Precision.HIGHEST
    cmul = lambda ar, ai, br, bi: (ar * br - ai * bi, ar * bi + ai * br)
    dt = jnp.exp(log_dt)[:, None]
    steps = jnp.arange(0, CHUNK + 1, dtype=F32)[:, None, None]
    mag = jnp.exp(lam_re[None] * (dt[None] * steps))
    ang = lam_im[None] * (dt[None] * steps)
    pw_r, pw_i = mag * jnp.cos(ang), mag * jnp.sin(ang)
    n_r, n_i = pw_r[1] - 1.0, pw_i[1]
    den = lam_re * lam_re + lam_im * lam_im
    q_r = (n_r * lam_re + n_i * lam_im) / den
    q_i = (n_i * lam_re - n_r * lam_im) / den
    bb_r, bb_i = cmul(q_r[..., None], q_i[..., None], b_re, b_im)
    e_r, e_i = cmul(pw_r[:CHUNK, :, :, None], pw_i[:CHUNK, :, :, None], bb_r[None], bb_i[None])
    k_tap = jnp.einsum("ghp,tgpk->tghk", jnp.concatenate([c_re, -c_im], axis=-1),
                       jnp.concatenate([e_r, e_i], axis=2), precision=hp)
    krow = k_tap.transpose(1, 3, 0, 2).reshape(SSM_GROUPS, SSM_GROUP, CHUNK_W).astype(BF16)
    kpad = jnp.concatenate([jnp.zeros_like(krow), krow], axis=-1)
    m_toep = jnp.stack([kpad[:, :, CHUNK_W - SSM_GROUP * c:2 * CHUNK_W - SSM_GROUP * c]
                        for c in range(CHUNK)], axis=1).reshape(SSM_GROUPS, CHUNK_W, CHUNK_W)
    rev_r, rev_i = pw_r[CHUNK - 1::-1][:CHUNK], pw_i[CHUNK - 1::-1][:CHUNK]
    w_r, w_i = cmul(rev_r[..., None], rev_i[..., None], bb_r[None], bb_i[None])
    lay = lambda w: w.transpose(1, 0, 3, 2).reshape(SSM_GROUPS, CHUNK_W, SSM_STATE)
    w_r, w_i = lay(w_r), lay(w_i)
    w_state = jnp.concatenate([w_r, w_i, w_i, w_r], axis=-1).astype(BF16)
    ca_r, ca_i = cmul(c_re[None], c_im[None], pw_r[1:CHUNK + 1, :, None, :],
                      pw_i[1:CHUNK + 1, :, None, :])
    lay = lambda w: w.transpose(1, 3, 0, 2).reshape(SSM_GROUPS, SSM_STATE, CHUNK_W)
    w_out = jnp.concatenate([lay(ca_r), -lay(ca_i)], axis=1).astype(BF16)
    ar, ai = pw_r[CHUNK], pw_i[CHUNK]
    ca = jnp.concatenate([ar, ar, ar, ar], axis=-1).reshape(8, -1)
    cb = jnp.concatenate([-ai, ai, ai, -ai], axis=-1).reshape(8, -1)
    return m_toep, w_state, w_out, ca, cb


def _mix_out_kernel(x_ref, ma_ref, xb_ref, ys_ref, d_ref, gw_ref, gb_ref, bgb_ref, wo_ref,
                    g1_ref, n2g_ref, sc2_ref, sh2_ref, xo_ref, h2_ref):
    y = ys_ref[...].astype(F32) + d_ref[...] * xb_ref[...]
    g = jax.nn.gelu(y)
    gate = jax.nn.sigmoid(jnp.dot(g.astype(BF16), gw_ref[...], preferred_element_type=F32)
                          + gb_ref[...])
    mb = _rms(g * gate, bgb_ref[...]).astype(BF16)
    mo = jnp.dot(ma_ref[...], wo_ref[:WIDTH_A, :], preferred_element_type=F32)
    mo += jnp.dot(mb, wo_ref[WIDTH_A:, :], preferred_element_type=F32)
    x = x_ref[...] + g1_ref[...] * mo
    xo_ref[...] = x
    h2_ref[...] = (_rms(x, n2g_ref[...]) * (1.0 + sc2_ref[...]) + sh2_ref[...]).astype(BF16)


def _mix_out_call(x, ma, xb, ys, d, glu_w, layer, glu_b, bgb, w_out, g1, n2g, sc2, sh2):
    tm = TM_MIX
    return pl.pallas_call(
        _mix_out_kernel, grid=(SEQ // tm,),
        in_specs=[_tile_spec(tm, D_MODEL), _tile_spec(tm, WIDTH_A), _tile_spec(tm, WIDTH_B),
                  _tile_spec(tm, WIDTH_B), _row_spec(WIDTH_B),
                  _layer_spec((WIDTH_B, WIDTH_B), layer), _row_spec(WIDTH_B), _row_spec(WIDTH_B),
                  _layer_spec((D_MODEL, D_MODEL), layer),
                  _row_spec(D_MODEL), _row_spec(D_MODEL), _row_spec(D_MODEL), _row_spec(D_MODEL)],
        out_specs=[_tile_spec(tm, D_MODEL), _tile_spec(tm, D_MODEL)],
        out_shape=[jax.ShapeDtypeStruct((SEQ, D_MODEL), F32),
                   jax.ShapeDtypeStruct((SEQ, D_MODEL), BF16)],
        compiler_params=_cparams(("arbitrary",)), name="mix_out",
    )(x, ma, xb, ys, d, glu_w, glu_b, bgb, w_out, g1, n2g, sc2, sh2)


_SORT_GROUP = 4
_ROW_BITS = 7


def _sorted_groups(x):
    n, w = x.shape
    assert n <= 1 << _ROW_BITS
    base = lax.broadcasted_iota(jnp.int32, (F32_SUBLANES, w), 0)
    groups = []
    for j in range(n // (F32_SUBLANES * _SORT_GROUP)):
        tile = lambda i: (j * _SORT_GROUP + i) * F32_SUBLANES
        v = [x[tile(i):tile(i) + F32_SUBLANES, :] for i in range(_SORT_GROUP)]
        r = [base + tile(i) for i in range(_SORT_GROUP)]

        def exchange(a, b, rows_ordered):
            if rows_ordered:
                first = v[a] >= v[b]
            else:
                first = jnp.logical_or(v[a] > v[b],
                                       jnp.logical_and(v[a] == v[b], r[a] < r[b]))
            v[a], v[b] = jnp.where(first, v[a], v[b]), jnp.where(first, v[b], v[a])
            r[a], r[b] = jnp.where(first, r[a], r[b]), jnp.where(first, r[b], r[a])

        exchange(0, 1, True)
        exchange(2, 3, True)
        exchange(0, 2, True)
        exchange(1, 3, True)
        exchange(1, 2, False)
        packed = r[0]
        for i in range(1, _SORT_GROUP):
            packed = packed | (r[i] << (i * _ROW_BITS))
        groups.append((tuple(v), packed))
    return tuple(groups)


def _rows_step(it, state, n, kiota):
    groups, vals, idxs = state
    last = _SORT_GROUP - 1
    heads = [v[0] for v, _ in groups]
    head_rows = [(rows & ((1 << _ROW_BITS) - 1)).astype(F32) for _, rows in groups]
    m = jnp.max(functools.reduce(jnp.maximum, heads), axis=0, keepdims=True)
    cands = [jnp.where(h == m, hr, float(n)) for h, hr in zip(heads, head_rows)]
    am = jnp.min(functools.reduce(jnp.minimum, cands), axis=0, keepdims=True)
    new_groups = []
    for (v, rows), hr in zip(groups, head_rows):
        won = hr == am
        nv = tuple(jnp.where(won, v[i + 1], v[i]) for i in range(last))
        nv += (jnp.where(won, _NEG_INF, v[last]),)
        new_groups.append((nv, jnp.where(won, rows >> _ROW_BITS, rows)))
    return (tuple(new_groups), jnp.where(kiota == it, m, vals), jnp.where(kiota == it, am, idxs))


def _pairs_step(it, state, lists, kiota, aiota):
    front, ptr, vals, ids = state
    v1, i1, v2, i2 = lists
    k = front.shape[0]
    m = jnp.max(front, axis=0, keepdims=True)
    a_sel = jnp.min(jnp.where(front == m, aiota, float(k)), axis=0, keepdims=True)
    won = aiota == a_sel
    b_sel = jnp.max(jnp.where(won, ptr, -1.0), axis=0, keepdims=True)
    e1 = jnp.max(jnp.where(won, i1, -1.0), axis=0, keepdims=True)
    e2 = jnp.max(jnp.where(aiota == b_sel, i2, -1.0), axis=0, keepdims=True)
    v2_next = jnp.max(jnp.where(aiota == b_sel + 1.0, v2, _NEG_INF), axis=0,
                      keepdims=True)
    return (jnp.where(won, v1 + v2_next, front), jnp.where(won, ptr + 1.0, ptr),
            jnp.where(kiota == it, m, vals),
            jnp.where(kiota == it, e1 * float(PEER_KEYS) + e2, ids))


def _select(xs, problems, k):
    n = xs[0].shape[0] if xs else 0
    w = (xs[0] if xs else problems[0][0]).shape[1]
    kiota = lax.broadcasted_iota(jnp.int32, (k, w), 0)
    aiota = kiota.astype(F32)

    def body(it, carry):
        rows_states, pair_states = carry
        return (tuple(_rows_step(it, s, n, kiota) for s in rows_states),
                tuple(_pairs_step(it, s, p, kiota, aiota) for s, p in zip(pair_states, problems)))

    zeros = jnp.zeros((k, w), F32)
    init = (tuple((_sorted_groups(x), zeros, zeros) for x in xs),
            tuple((v1 + v2[0:1, :], zeros, zeros, zeros) for v1, _, v2, _ in problems))
    rows_res, pair_res = lax.fori_loop(0, k, body, init)
    return ([(vals, idxs) for _, vals, idxs in rows_res],
            [(vals, ids) for _, _, vals, ids in pair_res])


def _peer_topk_kernel(h2_ref, wqt_ref, keys_ref, e_out_ref, g_out_ref, qt_ref, sc_ref, e_ref,
                      g_ref):
    qt_ref[...] = lax.dot_general(wqt_ref[...], h2_ref[...], (((1,), (1,)), ((), ())),
                                  preferred_element_type=F32).astype(BF16)
    for hs in range(2 * PEER_HEADS):
        rows = slice(hs * PEER_KEYS, (hs + 1) * PEER_KEYS)
        sc_ref[rows, :] = jnp.dot(keys_ref[hs], qt_ref[rows, :], preferred_element_type=F32)
    n_tiles = h2_ref.shape[0] // LANES
    tiles = [slice(t * LANES, (t + 1) * LANES) for t in range(n_tiles)]

    def emit(hh, pair_res):
        out0 = pl.multiple_of(hh * PEER_TOPK, PEER_TOPK)
        for lanes, (best, ids) in zip(tiles, pair_res):
            ex = jnp.exp(best - jnp.max(best, axis=0, keepdims=True))
            gate = ex / jnp.sum(ex, axis=0, keepdims=True)
            e_ref[pl.ds(out0, PEER_TOPK), lanes] = ids
            g_ref[pl.ds(out0, PEER_TOPK), lanes] = gate

    def head(hh, _):
        sides = []
        for s in range(2):
            row0 = pl.multiple_of(hh * (2 * PEER_KEYS) + s * PEER_KEYS, PEER_KEYS)
            sides.append(sc_ref[pl.ds(row0, PEER_KEYS), :])
        problems = []
        for lanes in tiles:
            (top1, top2), _ = _select([sides[0][:, lanes], sides[1][:, lanes]], [], PEER_TOPK)
            problems.append(top1 + top2)
        emit(hh, _select([], problems, PEER_TOPK)[1])
        return 0

    lax.fori_loop(0, PEER_HEADS, head, 0)

    for lanes in tiles:
        e_out_ref[lanes, :] = e_ref[:, lanes].T.astype(jnp.int32)
        g_out_ref[lanes, :] = g_ref[:, lanes].T


def _peer_topk_call(h2, wq_t, layer, keys):
    tq = TQ_TOPK
    n_slots = PEER_HEADS * PEER_TOPK
    return pl.pallas_call(
        _peer_topk_kernel, grid=(SEQ // tq,),
        in_specs=[_tile_spec(tq, D_MODEL), _layer_spec((D_MODEL, D_MODEL), layer),
                  _full_spec((2 * PEER_HEADS, PEER_KEYS, PEER_KEYS))],
        out_specs=[_tile_spec(tq, n_slots), _tile_spec(tq, n_slots)],
        out_shape=[jax.ShapeDtypeStruct((SEQ, n_slots), jnp.int32),
                   jax.ShapeDtypeStruct((SEQ, n_slots), F32)],
        scratch_shapes=[pltpu.VMEM((D_MODEL, tq), BF16), pltpu.VMEM((D_MODEL, tq), F32),
                        pltpu.VMEM((n_slots, tq), F32), pltpu.VMEM((n_slots, tq), F32)],
        compiler_params=_cparams(("arbitrary",)), name="peer_topk",
    )(h2, wq_t, keys)


def _peer_gate_kernel(e_ref, g_ref, o_ref):
    tg, n_slots = e_ref.shape
    kio = lax.broadcasted_iota(jnp.int32, (PEER_KEYS, n_slots), 0)

    def token(t, _):
        e = e_ref[pl.ds(t, 1), :]
        gate = g_ref[pl.ds(t, 1), :]
        i1 = e >> PEER_KEY_BITS
        i2 = e & (PEER_KEYS - 1)
        oh1 = jnp.where(i1 == kio, 1.0, 0.0).astype(BF16)
        w2 = jnp.where(i2 == kio, gate, 0.0).astype(BF16)
        o_ref[t] = lax.dot_general(oh1, w2, (((1,), (1,)), ((), ())),
                                   preferred_element_type=F32)
        return 0

    lax.fori_loop(0, tg, token, 0, unroll=GATE_UNROLL)


def _peer_gate_call(e_rows, g_rows):
    tg = TG_BUILD
    n_slots = PEER_HEADS * PEER_TOPK
    return pl.pallas_call(
        _peer_gate_kernel, grid=(SEQ // tg,),
        in_specs=[_tile_spec(tg, n_slots), _tile_spec(tg, n_slots)],
        out_specs=pl.BlockSpec((tg, PEER_KEYS, PEER_KEYS), lambda i: (i, 0, 0)),
        out_shape=jax.ShapeDtypeStruct((SEQ, PEER_KEYS, PEER_KEYS), F32),
        compiler_params=_cparams(("arbitrary",)), name="peer_gate",
    )(e_rows, g_rows)


def _peer_dense_kernel(h2_ref, u_ref, v_ref, g_ref, o_ref):
    @pl.when(pl.program_id(1) == 0)
    def _():
        o_ref[...] = jnp.zeros_like(o_ref)

    h2 = h2_ref[...]
    tb = h2_ref.shape[0]
    g_rows = g_ref.reshape(tb * KEY1_PER_BLOCK, PEER_KEYS)
    keys_per_chunk = EC_DENSE // PEER_KEYS
    n_chunks = EB_DENSE // EC_DENSE
    chunk = lambda c: slice(c * EC_DENSE, (c + 1) * EC_DENSE)
    score = lambda c: lax.dot_general(h2, u_ref[chunk(c), :], (((1,), (1,)), ((), ())),
                                      preferred_element_type=F32)
    a_next = score(0)
    for c in range(n_chunks):
        a = a_next
        if c + 1 < n_chunks:
            a_next = score(c + 1)
        gate = jnp.concatenate(
            [g_rows[pl.ds(c * keys_per_chunk + k, tb, stride=KEY1_PER_BLOCK), :]
             for k in range(keys_per_chunk)], axis=1)
        p = (gate * jax.nn.gelu(a)).astype(BF16)
        o_ref[...] += jnp.dot(p, v_ref[chunk(c), :], preferred_element_type=F32)


def _peer_dense_call(h2, u_tab, v_tab, layer, gmat):
    tb, eb = TB_DENSE, EB_DENSE
    return pl.pallas_call(
        _peer_dense_kernel, grid=(SEQ // tb, PEER_EXPERTS // eb),
        in_specs=[pl.BlockSpec((tb, D_MODEL), lambda i, j: (i, 0)),
                  pl.BlockSpec((None, eb, D_MODEL), lambda i, j: (layer, j, 0)),
                  pl.BlockSpec((None, eb, D_MODEL), lambda i, j: (layer, j, 0)),
                  pl.BlockSpec((tb, KEY1_PER_BLOCK, PEER_KEYS), lambda i, j: (i, j, 0))],
        out_specs=pl.BlockSpec((tb, D_MODEL), lambda i, j: (i, 0)),
        out_shape=jax.ShapeDtypeStruct((SEQ, D_MODEL), F32),
        compiler_params=_cparams(("arbitrary", "arbitrary")), name="peer_dense",
    )(h2, u_tab, v_tab, gmat)


def _final_kernel(x_ref, p_ref, g2_ref, fg_ref, o_ref):
    x = x_ref[...] + g2_ref[...] * p_ref[...]
    o_ref[...] = _rms(x, fg_ref[...])


def _final_call(x, peer, g2, fg):
    tm = TM_FINAL
    return pl.pallas_call(
        _final_kernel, grid=(SEQ // tm,),
        in_specs=[_tile_spec(tm, D_MODEL), _tile_spec(tm, D_MODEL), _row_spec(D_MODEL),
                  _row_spec(D_MODEL)],
        out_specs=_tile_spec(tm, D_MODEL),
        out_shape=jax.ShapeDtypeStruct((SEQ, D_MODEL), F32),
        compiler_params=_cparams(("arbitrary",)), name="final_norm",
    )(x, peer, g2, fg)


def _chunk_causal_mask(n):
    blk = jnp.arange(n) // CHUNK
    return blk[:, None] >= blk[None, :]


def kernel(x, c, w_mod, b_mod, norm1_g, w_in, b_in, gm_ln_g, gm_ln_b, gm_ws, gm_bs, ssm_lam_re, ssm_lam_im, ssm_log_dt, ssm_b_re, ssm_b_im, ssm_c_re, ssm_c_im, ssm_d, glu_w, glu_b, branch_g, w_out, norm2_g, peer_wq, peer_keys, peer_u, peer_v, final_g):
    assert x.shape == (1, SEQ, D_MODEL)
    xt = x.reshape(SEQ, D_MODEL)
    mod = _mod_call(c, w_mod, b_mod).reshape(DEPTH, N_MOD, 1, D_MODEL)
    row = lambda v: v.reshape(1, -1)
    res = None
    w_in_b, glu_w_b, w_out_b = w_in.astype(BF16), glu_w.astype(BF16), w_out.astype(BF16)
    wq_t_b = peer_wq.astype(BF16).transpose(0, 2, 1)
    u_b, v_b = peer_u.astype(BF16), peer_v.astype(BF16)
    for i in range(DEPTH):
        sh1, sc1, g1, sh2, sc2, g2 = (mod[i, k] for k in range(N_MOD))
        wm = jnp.where(_chunk_causal_mask(GMLP_BLOCK)[None], gm_ws[i], 0.0).astype(BF16)
        bs_full = jnp.repeat(gm_bs[i].T, LANES, axis=1)
        xt, ma, xb, xb16 = _mix_in_call(
            xt, res, row(norm1_g[i]), sc1, sh1, w_in_b, i, row(b_in[i]),
            row(gm_ln_g[i]), row(gm_ln_b[i]), wm, bs_full, row(branch_g[i, :WIDTH_A]))
        prep = _s5_prep(ssm_lam_re[i], ssm_lam_im[i], ssm_log_dt[i], ssm_b_re[i], ssm_b_im[i],
                        ssm_c_re[i], ssm_c_im[i])
        ys = _s5_call(xb16, prep)
        xt, h2 = _mix_out_call(
            xt, ma, xb, ys, row(ssm_d[i]), glu_w_b, i, row(glu_b[i]),
            row(branch_g[i, WIDTH_A:]), w_out_b, g1, row(norm2_g[i]), sc2, sh2)
        keys = peer_keys[i].reshape(2 * PEER_HEADS, PEER_KEYS, PEER_KEYS).astype(BF16)
        e_rows, gate_rows = _peer_topk_call(h2, wq_t_b, i, keys)
        gmat = _peer_gate_call(e_rows, gate_rows)
        peer = _peer_dense_call(h2, u_b, v_b, i, gmat)
        res = (peer, g2)
    out = _final_call(xt, res[0], res[1], row(final_g))
    return out.reshape(1, SEQ, D_MODEL)
```

```python
import functools

import jax
import jax.numpy as jnp
from jax import lax
from jax.experimental import pallas as pl
from jax.experimental.pallas import tpu as pltpu

F32 = jnp.float32
BF16 = jnp.bfloat16

D_MODEL = 2048
SEQ = 16384
DEPTH = 2
CHUNK = 64
WIDTH_A = 1024
WIDTH_B = 1024
IN_WIDTH = 3072
GMLP_BLOCK = 128
GMLP_HEADS = 8
SSM_GROUP = 16
SSM_GROUPS = 64
SSM_STATE = 64
PEER_HEADS = 8
PEER_KEYS = 128
PEER_KEY_BITS = PEER_KEYS.bit_length() - 1
assert 1 << PEER_KEY_BITS == PEER_KEYS
PEER_EXPERTS = PEER_KEYS * PEER_KEYS
PEER_TOPK = 16
N_MOD = 6
EPS = 1e-6

LANES = 128
N_CHUNKS = SEQ // CHUNK
CHUNK_W = CHUNK * SSM_GROUP
VMEM_LIMIT = 56 * 1024 * 1024

TM_MIX = 512
S5_GROUPS_PER_STEP = 2
TQ_TOPK = 512
TG_BUILD = 64
GATE_UNROLL = 64
TB_DENSE = 1024
EB_DENSE = 1024
EC_DENSE = 512
KEY1_PER_BLOCK = EB_DENSE // PEER_KEYS
F32_SUBLANES = 8
TM_FINAL = 512
TN_MOD = 1024

_NEG_INF = float("-inf")


def _cparams(sem):
    return pltpu.CompilerParams(dimension_semantics=sem, vmem_limit_bytes=VMEM_LIMIT)


def _rms(x, g):
    return x * lax.rsqrt(jnp.mean(x * x, axis=-1, keepdims=True) + EPS) * g


def _mod_kernel(c_ref, w_ref, b_ref, o_ref):
    c = c_ref[...]
    cond = c / (1.0 + jnp.exp(-c))
    w = w_ref[0]
    o_ref[0] = jnp.sum(w * cond, axis=0, keepdims=True) + b_ref[0]


def _mod_call(c, w_mod, b_mod):
    n = N_MOD * D_MODEL
    return pl.pallas_call(
        _mod_kernel,
        grid=(DEPTH, n // TN_MOD),
        in_specs=[
            pl.BlockSpec((D_MODEL, 1), lambda l, j: (0, 0)),
            pl.BlockSpec((1, D_MODEL, TN_MOD), lambda l, j: (l, 0, j)),
            pl.BlockSpec((1, 1, TN_MOD), lambda l, j: (l, 0, j)),
        ],
        out_specs=pl.BlockSpec((1, 1, TN_MOD), lambda l, j: (l, 0, j)),
        out_shape=jax.ShapeDtypeStruct((DEPTH, 1, n), F32),
        compiler_params=_cparams(("arbitrary", "arbitrary")),
        name="mod_gemv",
    )(c.reshape(D_MODEL, 1), w_mod, b_mod.reshape(DEPTH, 1, n))


def _mix_in_body(x, n1g, sc1, sh1, win_ref, bin_ref, lng, lnb, wm_ref, bs_ref, bga,
                 ma_ref, xb_ref, xb16_ref):
    h = (_rms(x, n1g) * (1.0 + sc1) + sh1).astype(BF16)
    proj = jnp.dot(h, win_ref[...], preferred_element_type=F32) + bin_ref[...]
    z = jax.nn.gelu(proj[:, :2 * WIDTH_A])
    u = z[:, :WIDTH_A]
    v = z[:, WIDTH_A:]
    mu = jnp.mean(v, axis=-1, keepdims=True)
    vc = v - mu
    var = jnp.mean(vc * vc, axis=-1, keepdims=True)
    vn = (vc * lax.rsqrt(var + EPS) * lng + lnb).astype(BF16)
    tm = x.shape[0]
    rows = []
    for n in range(tm // GMLP_BLOCK):
        vb = vn[n * GMLP_BLOCK:(n + 1) * GMLP_BLOCK, :]
        cols = []
        for hh in range(GMLP_HEADS):
            cols.append(jnp.dot(wm_ref[hh], vb[:, hh * LANES:(hh + 1) * LANES],
                                preferred_element_type=F32))
        rows.append(jnp.concatenate(cols, axis=1) + bs_ref[...])
    s = jnp.concatenate(rows, axis=0)
    out_a = u * s
    ma_ref[...] = _rms(out_a, bga).astype(BF16)
    xb = proj[:, 2 * WIDTH_A:]
    xb_ref[...] = xb
    xb16_ref[...] = xb.astype(BF16)


def _mix_in_kernel0(x_ref, n1g, sc1, sh1, win_ref, bin_ref, lng, lnb, wm_ref, bs_ref, bga,
                    ma_ref, xb_ref, xb16_ref):
    _mix_in_body(x_ref[...], n1g[...], sc1[...], sh1[...], win_ref, bin_ref, lng[...], lnb[...],
                 wm_ref, bs_ref, bga[...], ma_ref, xb_ref, xb16_ref)


def _mix_in_kernel1(x_ref, p_ref, g2_ref, n1g, sc1, sh1, win_ref, bin_ref, lng, lnb, wm_ref,
                    bs_ref, bga, xo_ref, ma_ref, xb_ref, xb16_ref):
    x = x_ref[...] + g2_ref[...] * p_ref[...]
    xo_ref[...] = x
    _mix_in_body(x, n1g[...], sc1[...], sh1[...], win_ref, bin_ref, lng[...], lnb[...],
                 wm_ref, bs_ref, bga[...], ma_ref, xb_ref, xb16_ref)


def _row_spec(width):
    return pl.BlockSpec((1, width), lambda i: (0, 0))


def _tile_spec(tm, width):
    return pl.BlockSpec((tm, width), lambda i: (i, 0))


def _full_spec(shape):
    nd = len(shape)
    return pl.BlockSpec(shape, lambda i: (0,) * nd)


def _layer_spec(shape, layer):
    nd = len(shape)
    return pl.BlockSpec((None,) + tuple(shape), lambda i: (layer,) + (0,) * nd,
                        pipeline_mode=pl.Buffered(1))


def _mix_in_call(x, res, n1g, sc1, sh1, w_in, layer, b_in, lng, lnb, wm, bs_full, bga):
    tm = TM_MIX
    common_specs = [
        _row_spec(D_MODEL), _row_spec(D_MODEL), _row_spec(D_MODEL),
        _layer_spec((D_MODEL, IN_WIDTH), layer), _row_spec(IN_WIDTH),
        _row_spec(WIDTH_A), _row_spec(WIDTH_A),
        _full_spec((GMLP_HEADS, GMLP_BLOCK, GMLP_BLOCK)),
        _full_spec((GMLP_BLOCK, WIDTH_A)),
        _row_spec(WIDTH_A),
    ]
    common_args = (n1g, sc1, sh1, w_in, b_in, lng, lnb, wm, bs_full, bga)
    out_specs = [_tile_spec(tm, WIDTH_A), _tile_spec(tm, WIDTH_B), _tile_spec(tm, WIDTH_B)]
    out_shape = [jax.ShapeDtypeStruct((SEQ, WIDTH_A), BF16),
                 jax.ShapeDtypeStruct((SEQ, WIDTH_B), F32),
                 jax.ShapeDtypeStruct((SEQ, WIDTH_B), BF16)]
    if res is None:
        ma, xb, xb16 = pl.pallas_call(
            _mix_in_kernel0, grid=(SEQ // tm,),
            in_specs=[_tile_spec(tm, D_MODEL)] + common_specs,
            out_specs=out_specs, out_shape=out_shape,
            compiler_params=_cparams(("arbitrary",)), name="mix_in",
        )(x, *common_args)
        return x, ma, xb, xb16
    peer, g2 = res
    xo, ma, xb, xb16 = pl.pallas_call(
        _mix_in_kernel1, grid=(SEQ // tm,),
        in_specs=[_tile_spec(tm, D_MODEL), _tile_spec(tm, D_MODEL), _row_spec(D_MODEL)]
        + common_specs,
        out_specs=[_tile_spec(tm, D_MODEL)] + out_specs,
        out_shape=[jax.ShapeDtypeStruct((SEQ, D_MODEL), F32)] + out_shape,
        compiler_params=_cparams(("arbitrary",)), name="mix_in_res",
    )(x, peer, g2, *common_args)
    return xo, ma, xb, xb16


def _s5_local_kernel(x_ref, w_ref, o_ref):
    width = w_ref.shape[2]
    for k in range(x_ref.shape[0]):
        o_ref[:, k * width:(k + 1) * width] = jnp.dot(x_ref[k], w_ref[k],
                                                      preferred_element_type=F32)


def _s5_scan_kernel(l_ref, ca_ref, cb_ref, h_ref):
    ca = ca_ref[...]
    cb = cb_ref[...]
    n_half = ca.shape[1] // LANES

    def step(k, s):
        h_ref[k] = s
        pieces = []
        for j in range(0, n_half, 2):
            pieces.append(s[:, (j + 1) * LANES:(j + 2) * LANES])
            pieces.append(s[:, j * LANES:(j + 1) * LANES])
        sw = jnp.concatenate(pieces, axis=1)
        return ca * s + cb * sw + l_ref[k]

    lax.fori_loop(0, N_CHUNKS, step, jnp.zeros(ca.shape, F32))


def _s5_out_kernel(x_ref, m_ref, h_ref, wo_ref, y_ref):
    state_w = wo_ref.shape[1]
    for k in range(x_ref.shape[0]):
        y = jnp.dot(x_ref[k], m_ref[k], preferred_element_type=F32)
        h = h_ref[:, 2 * k * state_w:(2 * k + 1) * state_w]
        h_hi = h.astype(BF16)
        h_lo = (h - h_hi.astype(F32)).astype(BF16)
        y += jnp.dot(h_hi, wo_ref[k], preferred_element_type=F32)
        y += jnp.dot(h_lo, wo_ref[k], preferred_element_type=F32)
        y_ref[k] = y.astype(y_ref.dtype)


def _s5_call(xb16, prep):
    m_toep, w_state, w_out, ca, cb = prep
    g = SSM_GROUPS
    xg = xb16.reshape(N_CHUNKS, CHUNK, g, SSM_GROUP).transpose(2, 0, 1, 3)
    xg = xg.reshape(g, N_CHUNKS, CHUNK_W)
    gs = S5_GROUPS_PER_STEP
    group_spec = lambda *shape: pl.BlockSpec((gs,) + shape, lambda i: (i, 0, 0))
    local = pl.pallas_call(
        _s5_local_kernel, grid=(g // gs,),
        in_specs=[group_spec(N_CHUNKS, CHUNK_W), group_spec(CHUNK_W, 4 * SSM_STATE)],
        out_specs=pl.BlockSpec((N_CHUNKS, gs * 4 * SSM_STATE), lambda i: (0, i)),
        out_shape=jax.ShapeDtypeStruct((N_CHUNKS, g * 4 * SSM_STATE), F32),
        compiler_params=_cparams(("arbitrary",)), name="s5_local",
    )(xg, w_state)
    rows = 8
    width = g * 4 * SSM_STATE // rows
    hstart = pl.pallas_call(
        _s5_scan_kernel,
        out_shape=jax.ShapeDtypeStruct((N_CHUNKS, rows, width), F32),
        compiler_params=pltpu.CompilerParams(vmem_limit_bytes=VMEM_LIMIT), name="s5_scan",
    )(local.reshape(N_CHUNKS, rows, width), ca, cb)
    hstart = hstart.reshape(N_CHUNKS, g * 4 * SSM_STATE)
    yg = pl.pallas_call(
        _s5_out_kernel, grid=(g // gs,),
        in_specs=[group_spec(N_CHUNKS, CHUNK_W),
                  group_spec(CHUNK_W, CHUNK_W),
                  pl.BlockSpec((N_CHUNKS, gs * 4 * SSM_STATE), lambda i: (0, i)),
                  group_spec(2 * SSM_STATE, CHUNK_W)],
        out_specs=group_spec(N_CHUNKS, CHUNK_W),
        out_shape=jax.ShapeDtypeStruct((g, N_CHUNKS, CHUNK_W), BF16),
        compiler_params=_cparams(("arbitrary",)), name="s5_out",
    )(xg, m_toep, hstart, w_out)
    y = yg.reshape(g, N_CHUNKS, CHUNK, SSM_GROUP).transpose(1, 2, 0, 3)
    return y.reshape(SEQ, WIDTH_B)


def _s5_prep(lam_re, lam_im, log_dt, b_re, b_im, c_re, c_im):
    hp = lax.Precision.HIGHEST
    cmul = lambda ar, ai, br, bi: (ar * br - ai * bi, ar * bi + ai * br)
    dt = jnp.exp(log_dt)[:, None]
    steps = jnp.arange(0, CHUNK + 1, dtype=F32)[:, None, None]
    mag = jnp.exp(lam_re[None] * (dt[None] * steps))
    ang = lam_im[None] * (dt[None] * steps)
    pw_r, pw_i = mag * jnp.cos(ang), mag * jnp.sin(ang)
    n_r, n_i = pw_r[1] - 1.0, pw_i[1]
    den = lam_re * lam_re + lam_im * lam_im
    q_r = (n_r * lam_re + n_i * lam_im) / den
    q_i = (n_i * lam_re - n_r * lam_im) / den
    bb_r, bb_i = cmul(q_r[..., None], q_i[..., None], b_re, b_im)
    e_r, e_i = cmul(pw_r[:CHUNK, :, :, None], pw_i[:CHUNK, :, :, None], bb_r[None], bb_i[None])
    k_tap = jnp.einsum("ghp,tgpk->tghk", jnp.concatenate([c_re, -c_im], axis=-1),
                       jnp.concatenate([e_r, e_i], axis=2), precision=hp)
    krow = k_tap.transpose(1, 3, 0, 2).reshape(SSM_GROUPS, SSM_GROUP, CHUNK_W).astype(BF16)
    kpad = jnp.concatenate([jnp.zeros_like(krow), krow], axis=-1)
    m_toep = jnp.stack([kpad[:, :, CHUNK_W - SSM_GROUP * c:2 * CHUNK_W - SSM_GROUP * c]
                        for c in range(CHUNK)], axis=1).reshape(SSM_GROUPS, CHUNK_W, CHUNK_W)
    rev_r, rev_i = pw_r[CHUNK - 1::-1][:CHUNK], pw_i[CHUNK - 1::-1][:CHUNK]
    w_r, w_i = cmul(rev_r[..., None], rev_i[..., None], bb_r[None], bb_i[None])
    lay = lambda w: w.transpose(1, 0, 3, 2).reshape(SSM_GROUPS, CHUNK_W, SSM_STATE)
    w_r, w_i = lay(w_r), lay(w_i)
    w_state = jnp.concatenate([w_r, w_i, w_i, w_r], axis=-1).astype(BF16)
    ca_r, ca_i = cmul(c_re[None], c_im[None], pw_r[1:CHUNK + 1, :, None, :],
                      pw_i[1:CHUNK + 1, :, None, :])
    lay = lambda w: w.transpose(1, 3, 0, 2).reshape(SSM_GROUPS, SSM_STATE, CHUNK_W)
    w_out = jnp.concatenate([lay(ca_r), -lay(ca_i)], axis=1).astype(BF16)
    ar, ai = pw_r[CHUNK], pw_i[CHUNK]
    ca = jnp.concatenate([ar, ar, ar, ar], axis=-1).reshape(8, -1)
    cb = jnp.concatenate([-ai, ai, ai, -ai], axis=-1).reshape(8, -1)
    return m_toep, w_state, w_out, ca, cb


def _mix_out_kernel(x_ref, ma_ref, xb_ref, ys_ref, d_ref, gw_ref, gb_ref, bgb_ref, wo_ref,
                    g1_ref, n2g_ref, sc2_ref, sh2_ref, xo_ref, h2_ref):
    y = ys_ref[...].astype(F32) + d_ref[...] * xb_ref[...]
    g = jax.nn.gelu(y)
    gate = jax.nn.sigmoid(jnp.dot(g.astype(BF16), gw_ref[...], preferred_element_type=F32)
                          + gb_ref[...])
    mb = _rms(g * gate, bgb_ref[...]).astype(BF16)
    mo = jnp.dot(ma_ref[...], wo_ref[:WIDTH_A, :], preferred_element_type=F32)
    mo += jnp.dot(mb, wo_ref[WIDTH_A:, :], preferred_element_type=F32)
    x = x_ref[...] + g1_ref[...] * mo
    xo_ref[...] = x
    h2_ref[...] = (_rms(x, n2g_ref[...]) * (1.0 + sc2_ref[...]) + sh2_ref[...]).astype(BF16)


def _mix_out_call(x, ma, xb, ys, d, glu_w, layer, glu_b, bgb, w_out, g1, n2g, sc2, sh2):
    tm = TM_MIX
    return pl.pallas_call(
        _mix_out_kernel, grid=(SEQ // tm,),
        in_specs=[_tile_spec(tm, D_MODEL), _tile_spec(tm, WIDTH_A), _tile_spec(tm, WIDTH_B),
                  _tile_spec(tm, WIDTH_B), _row_spec(WIDTH_B),
                  _layer_spec((WIDTH_B, WIDTH_B), layer), _row_spec(WIDTH_B), _row_spec(WIDTH_B),
                  _layer_spec((D_MODEL, D_MODEL), layer),
                  _row_spec(D_MODEL), _row_spec(D_MODEL), _row_spec(D_MODEL), _row_spec(D_MODEL)],
        out_specs=[_tile_spec(tm, D_MODEL), _tile_spec(tm, D_MODEL)],
        out_shape=[jax.ShapeDtypeStruct((SEQ, D_MODEL), F32),
                   jax.ShapeDtypeStruct((SEQ, D_MODEL), BF16)],
        compiler_params=_cparams(("arbitrary",)), name="mix_out",
    )(x, ma, xb, ys, d, glu_w, glu_b, bgb, w_out, g1, n2g, sc2, sh2)


_SORT_GROUP = 4
_ROW_BITS = 7


def _sorted_groups(x):
    n, w = x.shape
    assert n <= 1 << _ROW_BITS
    base = lax.broadcasted_iota(jnp.int32, (F32_SUBLANES, w), 0)
    groups = []
    for j in range(n // (F32_SUBLANES * _SORT_GROUP)):
        tile = lambda i: (j * _SORT_GROUP + i) * F32_SUBLANES
        v = [x[tile(i):tile(i) + F32_SUBLANES, :] for i in range(_SORT_GROUP)]
        r = [base + tile(i) for i in range(_SORT_GROUP)]

        def exchange(a, b, rows_ordered):
            if rows_ordered:
                first = v[a] >= v[b]
            else:
                first = jnp.logical_or(v[a] > v[b],
                                       jnp.logical_and(v[a] == v[b], r[a] < r[b]))
            v[a], v[b] = jnp.where(first, v[a], v[b]), jnp.where(first, v[b], v[a])
            r[a], r[b] = jnp.where(first, r[a], r[b]), jnp.where(first, r[b], r[a])

        exchange(0, 1, True)
        exchange(2, 3, True)
        exchange(0, 2, True)
        exchange(1, 3, True)
        exchange(1, 2, False)
        packed = r[0]
        for i in range(1, _SORT_GROUP):
            packed = packed | (r[i] << (i * _ROW_BITS))
        groups.append((tuple(v), packed))
    return tuple(groups)


def _rows_step(it, state, n, kiota):
    groups, vals, idxs = state
    last = _SORT_GROUP - 1
    heads = [v[0] for v, _ in groups]
    head_rows = [(rows & ((1 << _ROW_BITS) - 1)).astype(F32) for _, rows in groups]
    m = jnp.max(functools.reduce(jnp.maximum, heads), axis=0, keepdims=True)
    cands = [jnp.where(h == m, hr, float(n)) for h, hr in zip(heads, head_rows)]
    am = jnp.min(functools.reduce(jnp.minimum, cands), axis=0, keepdims=True)
    new_groups = []
    for (v, rows), hr in zip(groups, head_rows):
        won = hr == am
        nv = tuple(jnp.where(won, v[i + 1], v[i]) for i in range(last))
        nv += (jnp.where(won, _NEG_INF, v[last]),)
        new_groups.append((nv, jnp.where(won, rows >> _ROW_BITS, rows)))
    return (tuple(new_groups), jnp.where(kiota == it, m, vals), jnp.where(kiota == it, am, idxs))


def _pairs_step(it, state, lists, kiota, aiota):
    front, ptr, vals, ids = state
    v1, i1, v2, i2 = lists
    k = front.shape[0]
    m = jnp.max(front, axis=0, keepdims=True)
    a_sel = jnp.min(jnp.where(front == m, aiota, float(k)), axis=0, keepdims=True)
    won = aiota == a_sel
    b_sel = jnp.max(jnp.where(won, ptr, -1.0), axis=0, keepdims=True)
    e1 = jnp.max(jnp.where(won, i1, -1.0), axis=0, keepdims=True)
    e2 = jnp.max(jnp.where(aiota == b_sel, i2, -1.0), axis=0, keepdims=True)
    v2_next = jnp.max(jnp.where(aiota == b_sel + 1.0, v2, _NEG_INF), axis=0,
                      keepdims=True)
    return (jnp.where(won, v1 + v2_next, front), jnp.where(won, ptr + 1.0, ptr),
            jnp.where(kiota == it, m, vals),
            jnp.where(kiota == it, e1 * float(PEER_KEYS) + e2, ids))


def _select(xs, problems, k):
    n = xs[0].shape[0] if xs else 0
    w = (xs[0] if xs else problems[0][0]).shape[1]
    kiota = lax.broadcasted_iota(jnp.int32, (k, w), 0)
    aiota = kiota.astype(F32)

    def body(it, carry):
        rows_states, pair_states = carry
        return (tuple(_rows_step(it, s, n, kiota) for s in rows_states),
                tuple(_pairs_step(it, s, p, kiota, aiota) for s, p in zip(pair_states, problems)))

    zeros = jnp.zeros((k, w), F32)
    init = (tuple((_sorted_groups(x), zeros, zeros) for x in xs),
            tuple((v1 + v2[0:1, :], zeros, zeros, zeros) for v1, _, v2, _ in problems))
    rows_res, pair_res = lax.fori_loop(0, k, body, init)
    return ([(vals, idxs) for _, vals, idxs in rows_res],
            [(vals, ids) for _, _, vals, ids in pair_res])


def _peer_topk_kernel(h2_ref, wqt_ref, keys_ref, e_out_ref, g_out_ref, qt_ref, sc_ref, e_ref,
                      g_ref):
    qt_ref[...] = lax.dot_general(wqt_ref[...], h2_ref[...], (((1,), (1,)), ((), ())),
                                  preferred_element_type=F32).astype(BF16)
    for hs in range(2 * PEER_HEADS):
        rows = slice(hs * PEER_KEYS, (hs + 1) * PEER_KEYS)
        sc_ref[rows, :] = jnp.dot(keys_ref[hs], qt_ref[rows, :], preferred_element_type=F32)
    n_tiles = h2_ref.shape[0] // LANES
    tiles = [slice(t * LANES, (t + 1) * LANES) for t in range(n_tiles)]

    def emit(hh, pair_res):
        out0 = pl.multiple_of(hh * PEER_TOPK, PEER_TOPK)
        for lanes, (best, ids) in zip(tiles, pair_res):
            ex = jnp.exp(best - jnp.max(best, axis=0, keepdims=True))
            gate = ex / jnp.sum(ex, axis=0, keepdims=True)
            e_ref[pl.ds(out0, PEER_TOPK), lanes] = ids
            g_ref[pl.ds(out0, PEER_TOPK), lanes] = gate

    def head(hh, _):
        sides = []
        for s in range(2):
            row0 = pl.multiple_of(hh * (2 * PEER_KEYS) + s * PEER_KEYS, PEER_KEYS)
            sides.append(sc_ref[pl.ds(row0, PEER_KEYS), :])
        problems = []
        for lanes in tiles:
            (top1, top2), _ = _select([sides[0][:, lanes], sides[1][:, lanes]], [], PEER_TOPK)
            problems.append(top1 + top2)
        emit(hh, _select([], problems, PEER_TOPK)[1])
        return 0

    lax.fori_loop(0, PEER_HEADS, head, 0)

    for lanes in tiles:
        e_out_ref[lanes, :] = e_ref[:, lanes].T.astype(jnp.int32)
        g_out_ref[lanes, :] = g_ref[:, lanes].T


def _peer_topk_call(h2, wq_t, layer, keys):
    tq = TQ_TOPK
    n_slots = PEER_HEADS * PEER_TOPK
    return pl.pallas_call(
        _peer_topk_kernel, grid=(SEQ // tq,),
        in_specs=[_tile_spec(tq, D_MODEL), _layer_spec((D_MODEL, D_MODEL), layer),
                  _full_spec((2 * PEER_HEADS, PEER_KEYS, PEER_KEYS))],
        out_specs=[_tile_spec(tq, n_slots), _tile_spec(tq, n_slots)],
        out_shape=[jax.ShapeDtypeStruct((SEQ, n_slots), jnp.int32),
                   jax.ShapeDtypeStruct((SEQ, n_slots), F32)],
        scratch_shapes=[pltpu.VMEM((D_MODEL, tq), BF16), pltpu.VMEM((D_MODEL, tq), F32),
                        pltpu.VMEM((n_slots, tq), F32), pltpu.VMEM((n_slots, tq), F32)],
        compiler_params=_cparams(("arbitrary",)), name="peer_topk",
    )(h2, wq_t, keys)


def _peer_gate_kernel(e_ref, g_ref, o_ref):
    tg, n_slots = e_ref.shape
    kio = lax.broadcasted_iota(jnp.int32, (PEER_KEYS, n_slots), 0)

    def token(t, _):
        e = e_ref[pl.ds(t, 1), :]
        gate = g_ref[pl.ds(t, 1), :]
        i1 = e >> PEER_KEY_BITS
        i2 = e & (PEER_KEYS - 1)
        oh1 = jnp.where(i1 == kio, 1.0, 0.0).astype(BF16)
        w2 = jnp.where(i2 == kio, gate, 0.0).astype(BF16)
        o_ref[t] = lax.dot_general(oh1, w2, (((1,), (1,)), ((), ())),
                                   preferred_element_type=F32)
        return 0

    lax.fori_loop(0, tg, token, 0, unroll=GATE_UNROLL)


def _peer_gate_call(e_rows, g_rows):
    tg = TG_BUILD
    n_slots = PEER_HEADS * PEER_TOPK
    return pl.pallas_call(
        _peer_gate_kernel, grid=(SEQ // tg,),
        in_specs=[_tile_spec(tg, n_slots), _tile_spec(tg, n_slots)],
        out_specs=pl.BlockSpec((tg, PEER_KEYS, PEER_KEYS), lambda i: (i, 0, 0)),
        out_shape=jax.ShapeDtypeStruct((SEQ, PEER_KEYS, PEER_KEYS), F32),
        compiler_params=_cparams(("arbitrary",)), name="peer_gate",
    )(e_rows, g_rows)


def _peer_dense_kernel(h2_ref, u_ref, v_ref, g_ref, o_ref):
    @pl.when(pl.program_id(1) == 0)
    def _():
        o_ref[...] = jnp.zeros_like(o_ref)

    h2 = h2_ref[...]
    tb = h2_ref.shape[0]
    g_rows = g_ref.reshape(tb * KEY1_PER_BLOCK, PEER_KEYS)
    keys_per_chunk = EC_DENSE // PEER_KEYS
    n_chunks = EB_DENSE // EC_DENSE
    chunk = lambda c: slice(c * EC_DENSE, (c + 1) * EC_DENSE)
    score = lambda c: lax.dot_general(h2, u_ref[chunk(c), :], (((1,), (1,)), ((), ())),
                                      preferred_element_type=F32)
    a_next = score(0)
    for c in range(n_chunks):
        a = a_next
        if c + 1 < n_chunks:
            a_next = score(c + 1)
        gate = jnp.concatenate(
            [g_rows[pl.ds(c * keys_per_chunk + k, tb, stride=KEY1_PER_BLOCK), :]
             for k in range(keys_per_chunk)], axis=1)
        p = (gate * jax.nn.gelu(a)).astype(BF16)
        o_ref[...] += jnp.dot(p, v_ref[chunk(c), :], preferred_element_type=F32)


def _peer_dense_call(h2, u_tab, v_tab, layer, gmat):
    tb, eb = TB_DENSE, EB_DENSE
    return pl.pallas_call(
        _peer_dense_kernel, grid=(SEQ // tb, PEER_EXPERTS // eb),
        in_specs=[pl.BlockSpec((tb, D_MODEL), lambda i, j: (i, 0)),
                  pl.BlockSpec((None, eb, D_MODEL), lambda i, j: (layer, j, 0)),
                  pl.BlockSpec((None, eb, D_MODEL), lambda i, j: (layer, j, 0)),
                  pl.BlockSpec((tb, KEY1_PER_BLOCK, PEER_KEYS), lambda i, j: (i, j, 0))],
        out_specs=pl.BlockSpec((tb, D_MODEL), lambda i, j: (i, 0)),
        out_shape=jax.ShapeDtypeStruct((SEQ, D_MODEL), F32),
        compiler_params=_cparams(("arbitrary", "arbitrary")), name="peer_dense",
    )(h2, u_tab, v_tab, gmat)


def _final_kernel(x_ref, p_ref, g2_ref, fg_ref, o_ref):
    x = x_ref[...] + g2_ref[...] * p_ref[...]
    o_ref[...] = _rms(x, fg_ref[...])


def _final_call(x, peer, g2, fg):
    tm = TM_FINAL
    return pl.pallas_call(
        _final_kernel, grid=(SEQ // tm,),
        in_specs=[_tile_spec(tm, D_MODEL), _tile_spec(tm, D_MODEL), _row_spec(D_MODEL),
                  _row_spec(D_MODEL)],
        out_specs=_tile_spec(tm, D_MODEL),
        out_shape=jax.ShapeDtypeStruct((SEQ, D_MODEL), F32),
        compiler_params=_cparams(("arbitrary",)), name="final_norm",
    )(x, peer, g2, fg)


def _chunk_causal_mask(n):
    blk = jnp.arange(n) // CHUNK
    return blk[:, None] >= blk[None, :]


def kernel(x, c, w_mod, b_mod, norm1_g, w_in, b_in, gm_ln_g, gm_ln_b, gm_ws, gm_bs, ssm_lam_re, ssm_lam_im, ssm_log_dt, ssm_b_re, ssm_b_im, ssm_c_re, ssm_c_im, ssm_d, glu_w, glu_b, branch_g, w_out, norm2_g, peer_wq, peer_keys, peer_u, peer_v, final_g):
    assert x.shape == (1, SEQ, D_MODEL)
    xt = x.reshape(SEQ, D_MODEL)
    mod = _mod_call(c, w_mod, b_mod).reshape(DEPTH, N_MOD, 1, D_MODEL)
    row = lambda v: v.reshape(1, -1)
    res = None
    w_in_b, glu_w_b, w_out_b = w_in.astype(BF16), glu_w.astype(BF16), w_out.astype(BF16)
    wq_t_b = peer_wq.astype(BF16).transpose(0, 2, 1)
    u_b, v_b = peer_u.astype(BF16), peer_v.astype(BF16)
    for i in range(DEPTH):
        sh1, sc1, g1, sh2, sc2, g2 = (mod[i, k] for k in range(N_MOD))
        wm = jnp.where(_chunk_causal_mask(GMLP_BLOCK)[None], gm_ws[i], 0.0).astype(BF16)
        bs_full = jnp.repeat(gm_bs[i].T, LANES, axis=1)
        xt, ma, xb, xb16 = _mix_in_call(
            xt, res, row(norm1_g[i]), sc1, sh1, w_in_b, i, row(b_in[i]),
            row(gm_ln_g[i]), row(gm_ln_b[i]), wm, bs_full, row(branch_g[i, :WIDTH_A]))
        prep = _s5_prep(ssm_lam_re[i], ssm_lam_im[i], ssm_log_dt[i], ssm_b_re[i], ssm_b_im[i],
                        ssm_c_re[i], ssm_c_im[i])
        ys = _s5_call(xb16, prep)
        xt, h2 = _mix_out_call(
            xt, ma, xb, ys, row(ssm_d[i]), glu_w_b, i, row(glu_b[i]),
            row(branch_g[i, WIDTH_A:]), w_out_b, g1, row(norm2_g[i]), sc2, sh2)
        keys = peer_keys[i].reshape(2 * PEER_HEADS, PEER_KEYS, PEER_KEYS).astype(BF16)
        e_rows, gate_rows = _peer_topk_call(h2, wq_t_b, i, keys)
        gmat = _peer_gate_call(e_rows, gate_rows)
        peer = _peer_dense_call(h2, u_b, v_b, i, gmat)
        res = (peer, g2)
    out = _final_call(xt, res[0], res[1], row(final_g))
    return out.reshape(1, SEQ, D_MODEL)
```
